```python
import math
import jax
import jax.numpy as jnp
from jax import lax
import numpy as np

D_MODEL = 1024
BATCH = 8
SEQ = 8192
DEPTH = 4

N_META = 16
CHUNK = 64
PAD_FRONT = CHUNK - N_META
N_BRANCH = 4
W_MIX = D_MODEL // 4
HEAD_DIM = 64
N_HEADS = W_MIX // HEAD_DIM
MLSTM_CONV = 4
RANK_W = 64
RANK_A = 64
RANK_V = 32
RANK_G = 128
S5_GROUP = 16
S5_GROUPS = W_MIX // S5_GROUP
S5_STATE = 64
D_FF = 256 * ((8 * D_MODEL // 3 + 255) // 256)
FFN_CONV = 3
M_COLS = 4 * W_MIX + 2 * N_HEADS
R_COLS = 3 * W_MIX + RANK_W + RANK_A + RANK_G
H_COLS = 4 * W_MIX
S_COLS = W_MIX
IN_COLS = M_COLS + R_COLS + H_COLS + S_COLS
RMS_EPS = 1e-6
GN_EPS = 64e-5
L2_EPS = 1e-12
LB_FLOOR = 1e-30
NEG = -1e30
F32 = jnp.float32

kernel_name = 'hybrid_mlstm_rwkv7_hgrn2_s5_trunk'


def _split(a, sizes):
    return jnp.split(a, np.cumsum(sizes)[:-1].tolist(), axis=-1)


def _rmsnorm(x, g):
    xf = x.astype(F32)
    y = xf * lax.rsqrt(jnp.mean(xf * xf, axis=-1, keepdims=True) + RMS_EPS)
    return (y * g.astype(F32)).astype(x.dtype)


def _head_rmsnorm(y, g):
    y = y * lax.rsqrt(jnp.mean(y * y, axis=-1, keepdims=True) + RMS_EPS)
    return y * g.astype(F32).reshape(N_HEADS, HEAD_DIM)


def _causal_dwconv(x, w, b):
    k, c = w.shape
    y = lax.conv_general_dilated(x, w.astype(x.dtype)[:, None, :], window_strides=(1,),
                                 padding=[(k - 1, 0)], dimension_numbers=('NWC', 'WIO', 'NWC'),
                                 feature_group_count=c)
    return y + b.astype(x.dtype)


def _heads(a):
    b, t, _ = a.shape
    return a.astype(F32).reshape(b, t, N_HEADS, HEAD_DIM).transpose(0, 2, 1, 3)


def _pad_front(a, value):
    pad = [(0, 0)] * a.ndim
    pad[2] = (PAD_FRONT, 0)
    return jnp.pad(a, pad, constant_values=value)


def _to_chunks(a):
    b, hh, l = a.shape[:3]
    a = a.reshape((b, hh, l // CHUNK, CHUNK) + a.shape[3:])
    return jnp.moveaxis(a, 2, 0)


def _from_chunks(a):
    a = jnp.moveaxis(a, 0, 2)
    a = a.reshape(a.shape[:2] + (a.shape[2] * a.shape[3],) + a.shape[4:])
    return a[:, :, PAD_FRONT:]


def _causal_mask():
    return jnp.tril(jnp.ones((CHUNK, CHUNK), dtype=bool))


def _mlstm_chunk(carry, xs):
    c_st, n_st, m_st = carry
    q, k, v, li, lf = xs
    b = jnp.cumsum(lf, axis=-1)
    dmat = jnp.where(_causal_mask(), b[..., :, None] - b[..., None, :] + li[..., None, :], NEG)
    inter = b + m_st[..., None]
    m = jnp.maximum(inter, jnp.max(dmat, axis=-1))
    wmat = jnp.exp(dmat - m[..., None])
    sc = jnp.exp(inter - m)
    s = jnp.einsum('bhtd,bhsd->bhts', q, k) * wmat
    num = sc[..., None] * jnp.einsum('bhtd,bhde->bhte', q, c_st) + jnp.einsum('bhts,bhse->bhte', s, v)
    den = sc * jnp.einsum('bhtd,bhd->bht', q, n_st) + jnp.sum(s, axis=-1)
    h = num / jnp.maximum(jnp.abs(den), jnp.exp(-m))[..., None]
    b_last = b[..., -1]
    g = b_last[..., None] - b + li
    m_new = jnp.maximum(b_last + m_st, jnp.max(g, axis=-1))
    ws = jnp.exp(g - m_new[..., None])
    dec = jnp.exp(b_last + m_st - m_new)
    c_new = dec[..., None, None] * c_st + jnp.einsum('bhsd,bhse->bhde', k * ws[..., None], v)
    n_new = dec[..., None] * n_st + jnp.einsum('bhs,bhsd->bhd', ws, k)
    return (c_new, n_new, m_new), h


def _mlstm(p, conv_w, conv_b, gate_b, norm_g):
    dt = p.dtype
    bsz, t, _ = p.shape
    q, k, v, o, ig, fg = _split(p, [W_MIX] * 4 + [N_HEADS] * 2)
    qk = jax.nn.silu(_causal_dwconv(jnp.concatenate([q, k], axis=-1), conv_w, conv_b))
    q, k = jnp.split(qk, 2, axis=-1)
    log_i = (ig + gate_b[0]).astype(F32).transpose(0, 2, 1)
    log_f = jax.nn.log_sigmoid((fg + gate_b[1]).astype(F32)).transpose(0, 2, 1)
    qh = _heads(q) * (HEAD_DIM ** -0.5)
    kh = _heads(k)
    vh = _heads(v)
    xs = (_to_chunks(_pad_front(qh, 0.0)), _to_chunks(_pad_front(kh, 0.0)), _to_chunks(_pad_front(vh, 0.0)),
          _to_chunks(_pad_front(log_i, NEG)), _to_chunks(_pad_front(log_f, 0.0)))
    init = (jnp.zeros((bsz, N_HEADS, HEAD_DIM, HEAD_DIM), F32), jnp.zeros((bsz, N_HEADS, HEAD_DIM), F32),
            jnp.zeros((bsz, N_HEADS), F32))
    _, hs = lax.scan(_mlstm_chunk, init, xs)
    hs = _from_chunks(hs).transpose(0, 2, 1, 3)
    y = _head_rmsnorm(hs, norm_g).reshape(bsz, t, W_MIX)
    return (jax.nn.sigmoid(o.astype(F32)) * y).astype(dt)


def _rwkv7_step(s_st, xs):
    r, w, k, v, a, b = xs
    sa = jnp.einsum('bhij,bhj->bhi', s_st, a)
    s_st = s_st * w[:, :, None, :] + sa[..., None] * b[:, :, None, :] + v[..., None] * k[:, :, None, :]
    return s_st, jnp.einsum('bhij,bhj->bhi', s_st, r)


def _rwkv7(p, mu, w0, w2, a0, a2, g2, k_k, k_a, r_k, gn_w, gn_b, v_first, v_res):
    dt = p.dtype
    bsz, t, _ = p.shape
    shifted = jnp.pad(p[:, :-1], ((0, 0), (1, 0), (0, 0)))
    xm = p + (shifted - p) * mu
    r, k, v, cw, ca, cg = _split(xm, [W_MIX] * 3 + [RANK_W, RANK_A, RANK_G])
    w = -jax.nn.softplus(-(w0 + jnp.tanh(cw) @ w2)) - 0.5
    if v_res is None:
        v_first = v
    else:
        v0, v1, v2 = v_res
        v = v + (v_first - v) * jax.nn.sigmoid(v0 + (v @ v1) @ v2)
    a = jax.nn.sigmoid(a0 + ca @ a2)
    g = jax.nn.sigmoid(cg) @ g2

    def hd(z):
        return z.astype(F32).reshape(bsz, t, N_HEADS, HEAD_DIM)

    kk = hd(k * k_k)
    kk = kk / jnp.maximum(jnp.sqrt(jnp.sum(kk * kk, axis=-1, keepdims=True)), L2_EPS)
    k = k * (1.0 + (a - 1.0) * k_a)
    rh, kh, vh, ah = hd(r), hd(k), hd(v), hd(a)
    decay = jnp.exp(-jnp.exp(hd(w)))

    def tm(z):
        return jnp.moveaxis(z, 1, 0)

    xs = (tm(rh), tm(decay), tm(kh), tm(vh), tm(-kk), tm(kk * ah))
    _, y = lax.scan(_rwkv7_step, jnp.zeros((bsz, N_HEADS, HEAD_DIM, HEAD_DIM), F32), xs)
    y = jnp.moveaxis(y, 0, 1)
    mean = jnp.mean(y, axis=-1, keepdims=True)
    var = jnp.mean(jnp.square(y - mean), axis=-1, keepdims=True)
    y = (y - mean) * lax.rsqrt(var + GN_EPS) * gn_w.astype(F32).reshape(N_HEADS, HEAD_DIM) \
        + gn_b.astype(F32).reshape(N_HEADS, HEAD_DIM)
    bonus = jnp.sum(rh * kh * r_k.astype(F32).reshape(N_HEADS, HEAD_DIM), axis=-1, keepdims=True)
    y = (y + bonus * vh).reshape(bsz, t, W_MIX) * g.astype(F32)
    return y.astype(dt), v_first


def _hgrn2_chunk(s_st, xs):
    q, k, v, lf = xs
    g = jnp.cumsum(lf, axis=2)
    o_inter = jnp.einsum('bhtd,bhde->bhte', q * jnp.exp(g), s_st)
    diff = jnp.where(_causal_mask()[:, :, None], g[:, :, :, None, :] - g[:, :, None, :, :], NEG)
    att = jnp.sum(q[:, :, :, None, :] * k[:, :, None, :, :] * jnp.exp(diff), axis=-1)
    o = o_inter + jnp.einsum('bhts,bhse->bhte', att, v)
    g_last = g[:, :, -1:, :]
    s_new = jnp.exp(g_last[:, :, 0, :, None]) * s_st + jnp.einsum('bhsd,bhse->bhde', k * jnp.exp(g_last - g), v)
    return s_new, o


def _hgrn2(p, lb, norm_g):
    dt = p.dtype
    bsz, t, _ = p.shape
    q, f, i, g = _split(p, [W_MIX] * 4)
    z = f.astype(F32)
    lb = lb.astype(F32)
    log_lb = jnp.log(jnp.maximum(lb, LB_FLOOR))
    log_f = jnp.logaddexp(log_lb, jnp.log1p(-lb) + jax.nn.log_sigmoid(z))
    k = (1.0 - lb) * jax.nn.sigmoid(-z)
    xs = (_to_chunks(_pad_front(_heads(jax.nn.silu(q)), 0.0)), _to_chunks(_pad_front(_heads(k), 0.0)),
          _to_chunks(_pad_front(_heads(i), 0.0)), _to_chunks(_pad_front(_heads(log_f), 0.0)))
    _, o = lax.scan(_hgrn2_chunk, jnp.zeros((bsz, N_HEADS, HEAD_DIM, HEAD_DIM), F32), xs)
    o = _from_chunks(o).transpose(0, 2, 1, 3)
    y = _head_rmsnorm(o, norm_g).reshape(bsz, t, W_MIX)
    return (y * jax.nn.silu(g.astype(F32))).astype(dt)


def _s5_combine(e1, e2):
    a1r, a1i, b1r, b1i = e1
    a2r, a2i, b2r, b2i = e2
    return (a2r * a1r - a2i * a1i, a2r * a1i + a2i * a1r,
            a2r * b1r - a2i * b1i + b2r, a2r * b1i + a2i * b1r + b2i)


def _s5(u, a_re, a_im, b_re, b_im, c_re, c_im, d_skip, log_step, w_glu, b_glu):
    dt = u.dtype
    bsz, t, _ = u.shape
    uf = u.astype(F32)
    ug = uf.reshape(bsz, t, S5_GROUPS, S5_GROUP)
    a_re = a_re.astype(F32)
    a_im = a_im.astype(F32)
    step = jnp.exp(log_step.astype(F32))[:, None]
    mag = jnp.exp(a_re * step)
    ang = a_im * step
    ab_re = mag * jnp.cos(ang)
    ab_im = mag * jnp.sin(ang)
    den = a_re * a_re + a_im * a_im
    num_re = ab_re - 1.0
    coef_re = (num_re * a_re + ab_im * a_im) / den
    coef_im = (ab_im * a_re - num_re * a_im) / den
    b_re = b_re.astype(F32)
    b_im = b_im.astype(F32)
    bb_re = coef_re[..., None] * b_re - coef_im[..., None] * b_im
    bb_im = coef_re[..., None] * b_im + coef_im[..., None] * b_re
    bu_re = jnp.einsum('btgc,gpc->btgp', ug, bb_re)
    bu_im = jnp.einsum('btgc,gpc->btgp', ug, bb_im)
    shape = (1, t, S5_GROUPS, S5_STATE)
    elems = (jnp.broadcast_to(ab_re, shape), jnp.broadcast_to(ab_im, shape), bu_re, bu_im)
    _, _, s_re, s_im = lax.associative_scan(_s5_combine, elems, axis=1)
    y = jnp.einsum('btgp,gcp->btgc', s_re, c_re.astype(F32)) - jnp.einsum('btgp,gcp->btgc', s_im, c_im.astype(F32))
    y = y.reshape(bsz, t, W_MIX) + d_skip.astype(F32) * uf
    y = jax.nn.gelu(y)
    y = y * jax.nn.sigmoid(y @ w_glu.astype(F32) + b_glu.astype(F32))
    return y.astype(dt)


def _conv_ffn(u, w_up, conv_w, conv_b, w_down):
    z = _causal_dwconv(u @ w_up, conv_w, conv_b)
    gate, val = jnp.split(z, 2, axis=-1)
    return (jax.nn.gelu(gate) * val) @ w_down


def setup_inputs(seed: int = 0) -> dict:
    key = jax.random.key(seed)
    keys = iter(jax.random.split(key, 64))
    L = DEPTH

    def nrm(shape, scale):
        return scale * jax.random.normal(next(keys), shape, F32)

    x = nrm((BATCH, SEQ, D_MODEL), 1.0)
    meta = nrm((N_META, D_MODEL), 1.0)
    norms = 1.0 + nrm((L, 4, D_MODEL), 0.02)
    w_in = nrm((L, D_MODEL, IN_COLS), D_MODEL ** -0.5)
    w_gate = nrm((L, N_BRANCH, D_MODEL, D_MODEL), D_MODEL ** -0.5)
    b_gate = nrm((L, N_BRANCH, D_MODEL), 0.02)
    w_branch = nrm((L, N_BRANCH, W_MIX, D_MODEL), W_MIX ** -0.5)
    w_out = nrm((L, D_MODEL, D_MODEL), D_MODEL ** -0.5)
    m_conv_w = nrm((L, MLSTM_CONV, 2 * W_MIX), MLSTM_CONV ** -0.5)
    m_conv_b = nrm((L, 2 * W_MIX), 0.02)
    f_bias = jnp.linspace(3.0, 6.0, N_HEADS, dtype=F32)
    m_gate_b = jnp.stack([nrm((L, N_HEADS), 0.1), f_bias + nrm((L, N_HEADS), 0.1)], axis=1)
    m_norm = 1.0 + nrm((L, W_MIX), 0.02)
    r_mu = jax.random.uniform(next(keys), (L, R_COLS), F32)
    r_w0 = jnp.linspace(-6.0, -1.0, W_MIX, dtype=F32) + nrm((L, W_MIX), 0.1)
    r_w2 = nrm((L, RANK_W, W_MIX), 0.5 * RANK_W ** -0.5)
    r_a0 = nrm((L, W_MIX), 0.1)
    r_a2 = nrm((L, RANK_A, W_MIX), 0.5 * RANK_A ** -0.5)
    r_g2 = nrm((L, RANK_G, W_MIX), RANK_G ** -0.5)
    r_kk = 0.85 + nrm((L, W_MIX), 0.02)
    r_ka = 1.0 + nrm((L, W_MIX), 0.02)
    r_rk = nrm((L, W_MIX), 0.1)
    r_gn_w = 1.0 + nrm((L, W_MIX), 0.02)
    r_gn_b = nrm((L, W_MIX), 0.02)
    r_v0 = 1.0 + nrm((L - 1, W_MIX), 0.1)
    r_v1 = nrm((L - 1, W_MIX, RANK_V), W_MIX ** -0.5)
    r_v2 = nrm((L - 1, RANK_V, W_MIX), 0.5 * RANK_V ** -0.5)
    h_lb = 1.0 + nrm((L, W_MIX), 0.1)
    h_norm = 1.0 + nrm((L, W_MIX), 0.02)
    s_a_re = -0.5 + nrm((L, S5_GROUPS, S5_STATE), 0.01)
    s_a_im = math.pi * jnp.arange(S5_STATE, dtype=F32) + nrm((L, S5_GROUPS, S5_STATE), 0.01)
    s_b_re = nrm((L, S5_GROUPS, S5_STATE, S5_GROUP), (2 * S5_GROUP) ** -0.5)
    s_b_im = nrm((L, S5_GROUPS, S5_STATE, S5_GROUP), (2 * S5_GROUP) ** -0.5)
    s_c_re = nrm((L, S5_GROUPS, S5_GROUP, S5_STATE), S5_STATE ** -0.5)
    s_c_im = nrm((L, S5_GROUPS, S5_GROUP, S5_STATE), S5_STATE ** -0.5)
    s_d = nrm((L, W_MIX), 1.0)
    lo, hi = math.log(1e-3), math.log(1e-1)
    s_log_step = lo + (hi - lo) * jax.random.uniform(next(keys), (L, S5_GROUPS), F32)
    s_w_glu = nrm((L, W_MIX, W_MIX), W_MIX ** -0.5)
    s_b_glu = nrm((L, W_MIX), 0.02)
    f_up = nrm((L, D_MODEL, 2 * D_FF), D_MODEL ** -0.5)
    f_conv_w = nrm((L, FFN_CONV, 2 * D_FF), FFN_CONV ** -0.5)
    f_conv_b = nrm((L, 2 * D_FF), 0.02)
    f_down = nrm((L, D_FF, D_MODEL), D_FF ** -0.5)
    return {'x': x, 'meta': meta, 'norms': norms, 'w_in': w_in, 'w_gate': w_gate, 'b_gate': b_gate,
            'w_branch': w_branch, 'w_out': w_out, 'm_conv_w': m_conv_w, 'm_conv_b': m_conv_b,
            'm_gate_b': m_gate_b, 'm_norm': m_norm, 'r_mu': r_mu, 'r_w0': r_w0, 'r_w2': r_w2,
            'r_a0': r_a0, 'r_a2': r_a2, 'r_g2': r_g2, 'r_kk': r_kk, 'r_ka': r_ka, 'r_rk': r_rk,
            'r_gn_w': r_gn_w, 'r_gn_b': r_gn_b, 'r_v0': r_v0, 'r_v1': r_v1, 'r_v2': r_v2,
            'h_lb': h_lb, 'h_norm': h_norm, 's_a_re': s_a_re, 's_a_im': s_a_im, 's_b_re': s_b_re,
            's_b_im': s_b_im, 's_c_re': s_c_re, 's_c_im': s_c_im, 's_d': s_d, 's_log_step': s_log_step,
            's_w_glu': s_w_glu, 's_b_glu': s_b_glu, 'f_up': f_up, 'f_conv_w': f_conv_w,
            'f_conv_b': f_conv_b, 'f_down': f_down}


def reference(x, meta, norms, w_in, w_gate, b_gate, w_branch, w_out, m_conv_w, m_conv_b, m_gate_b, m_norm,
              r_mu, r_w0, r_w2, r_a0, r_a2, r_g2, r_kk, r_ka, r_rk, r_gn_w, r_gn_b, r_v0, r_v1, r_v2,
              h_lb, h_norm, s_a_re, s_a_im, s_b_re, s_b_im, s_c_re, s_c_im, s_d, s_log_step, s_w_glu, s_b_glu,
              f_up, f_conv_w, f_conv_b, f_down):
    dt = x.dtype
    bsz = x.shape[0]
    h = jnp.concatenate([jnp.broadcast_to(meta.astype(dt)[None], (bsz, N_META, D_MODEL)), x], axis=1)
    lb_w = jax.nn.softmax(h_lb.astype(F32), axis=0)
    lbs = jnp.cumsum(lb_w, axis=0) - lb_w[0:1]
    v_first = None
    for l in range(DEPTH):
        u = _rmsnorm(h, norms[l, 0])
        p_m, p_r, p_h, p_s = _split(u @ w_in[l], [M_COLS, R_COLS, H_COLS, S_COLS])
        y_m = _mlstm(p_m, m_conv_w[l], m_conv_b[l], m_gate_b[l], m_norm[l])
        v_res = None if l == 0 else (r_v0[l - 1], r_v1[l - 1], r_v2[l - 1])
        y_r, v_first = _rwkv7(p_r, r_mu[l], r_w0[l], r_w2[l], r_a0[l], r_a2[l], r_g2[l], r_kk[l], r_ka[l],
                              r_rk[l], r_gn_w[l], r_gn_b[l], v_first, v_res)
        y_h = _hgrn2(p_h, lbs[l], h_norm[l])
        y_s = _s5(p_s, s_a_re[l], s_a_im[l], s_b_re[l], s_b_im[l], s_c_re[l], s_c_im[l], s_d[l],
                  s_log_step[l], s_w_glu[l], s_b_glu[l])
        ys = (y_m, y_r, y_h, y_s)
        merged = None
        for n_b in range(N_BRANCH):
            gate = jax.nn.sigmoid(u @ w_gate[l, n_b] + b_gate[l, n_b])
            term = gate * (ys[n_b] @ w_branch[l, n_b])
            merged = term if merged is None else merged + term
        h = h + _rmsnorm(merged @ w_out[l], norms[l, 1])
        u2 = _rmsnorm(h, norms[l, 2])
        h = h + _rmsnorm(_conv_ffn(u2, f_up[l], f_conv_w[l], f_conv_b[l], f_down[l]), norms[l, 3])
    return h[:, N_META:]
```

```python
import functools
import math

import jax
import jax.numpy as jnp
from jax import lax
from jax.experimental import pallas as pl
from jax.experimental.pallas import tpu as pltpu

F32 = jnp.float32
BF16 = jnp.bfloat16

D_MODEL = 1024
N_META = 16
CHUNK = 64
PAD_FRONT = CHUNK - N_META
N_BRANCH = 4
W_MIX = D_MODEL // 4
HEAD_DIM = 64
N_HEADS = W_MIX // HEAD_DIM
MLSTM_CONV = 4
RANK_W = 64
RANK_A = 64
RANK_V = 32
RANK_G = 128
S5_GROUP = 16
S5_GROUPS = W_MIX // S5_GROUP
S5_STATE = 64
S5_N = S5_GROUPS * S5_STATE
D_FF = 256 * ((8 * D_MODEL // 3 + 255) // 256)
FFN_CONV = 3
M_COLS = 4 * W_MIX + 2 * N_HEADS
R_COLS = 3 * W_MIX + RANK_W + RANK_A + RANK_G
H_COLS = 4 * W_MIX
S_COLS = W_MIX
PM_COLS = 6 * W_MIX
RMS_EPS = 1e-6
GN_EPS = 64e-5
L2_EPS = 1e-12
LB_FLOOR = 1e-30
NEG = -1e30
SUB = 16

VMEM_LIMIT_BYTES = 56 * 1024 * 1024


def _dot(a, b):
    return jnp.dot(a.astype(BF16), b.astype(BF16), preferred_element_type=F32)


def _dot_nt(a, b):
    return lax.dot_general(a.astype(BF16), b.astype(BF16), (((1,), (1,)), ((), ())),
                           preferred_element_type=F32)


def _dot_tn(a, b):
    return lax.dot_general(a.astype(BF16), b.astype(BF16), (((0,), (0,)), ((), ())),
                           preferred_element_type=F32)


def _dot01(sel, x):
    hi = x.astype(BF16)
    r1 = x - hi.astype(F32)
    mid = r1.astype(BF16)
    lo = (r1 - mid.astype(F32)).astype(BF16)
    out = jnp.dot(sel, hi, preferred_element_type=F32)
    out = out + jnp.dot(sel, mid, preferred_element_type=F32)
    return out + jnp.dot(sel, lo, preferred_element_type=F32)


def _sigmoid(x):
    return 1.0 / (1.0 + jnp.exp(-x))


def _softplus(x):
    return jnp.maximum(x, 0.0) + jnp.log1p(jnp.exp(-jnp.abs(x)))


def _log_sigmoid(x):
    return -_softplus(-x)


def _silu(x):
    return x * _sigmoid(x)


def _gelu_tanh(x):
    c = math.sqrt(2.0 / math.pi)
    return 0.5 * x * (1.0 + jnp.tanh(c * (x + 0.044715 * (x * x * x))))


def _rmsnorm_rows(x, g):
    ms = jnp.mean(x * x, axis=-1, keepdims=True)
    return x * lax.rsqrt(ms + RMS_EPS) * g


def _iota(shape, dim):
    return lax.broadcasted_iota(jnp.int32, shape, dim)


def _head_consts():
    r = _iota((W_MIX, W_MIX), 0)
    c = _iota((W_MIX, W_MIX), 1)
    bd = (r >> 6) == (c >> 6)
    t = _iota((CHUNK, W_MIX), 0)
    s = _iota((CHUNK, W_MIX), 1) & (HEAD_DIM - 1)
    tri = (_iota((CHUNK, CHUNK), 0) >= _iota((CHUNK, CHUNK), 1)).astype(BF16)
    return bd, bd.astype(BF16), t, s, tri


def _tile_heads(x, bd):
    return jnp.where(bd, jnp.concatenate([x, x, x, x], axis=0), 0.0)


def _head_mean(x, bones):
    return _dot(x, bones) * (1.0 / HEAD_DIM)


def _row_tile(tp, target):
    best = None
    for cand in range(16, tp + 1, 16):
        if tp % cand == 0 and cand <= target:
            best = cand
    return best if best is not None else tp


def _chunk_block(tp, target):
    best = CHUNK
    for cand in range(CHUNK, tp + 1, CHUNK):
        if tp % cand == 0 and cand <= target:
            best = cand
    return best


def _const_spec(shape):
    nd = len(shape)
    return pl.BlockSpec(shape, lambda *_: (0,) * nd)


def _params(sem):
    return pltpu.CompilerParams(dimension_semantics=sem, vmem_limit_bytes=VMEM_LIMIT_BYTES)


def _proj_in_kernel(x_ref, g_ref, w_ref, pm_ref, pr_ref, ph_ref, ps_ref, *, tm):
    i = pl.program_id(1)
    x = x_ref[...]
    u = _rmsnorm_rows(x, g_ref[...])
    rows = i * tm + _iota((tm, 1), 0)
    ub = jnp.where(rows >= PAD_FRONT, u, 0.0).astype(BF16)
    c0 = 0
    for ref, n in ((pm_ref, PM_COLS), (pr_ref, R_COLS), (ph_ref, H_COLS), (ps_ref, S_COLS)):
        ref[...] = jnp.dot(ub, w_ref[:, c0:c0 + n], preferred_element_type=F32)
        c0 += n


def _proj_in(h, g, w, tm):
    bsz, tp, d = h.shape
    ncols = w.shape[1]
    row = lambda n: pl.BlockSpec((None, tm, n), lambda b, i: (b, i, 0))
    return pl.pallas_call(
        functools.partial(_proj_in_kernel, tm=tm),
        grid=(bsz, tp // tm),
        in_specs=[row(d), _const_spec((1, d)), _const_spec((d, ncols))],
        out_specs=[row(PM_COLS), row(R_COLS), row(H_COLS),
                   pl.BlockSpec((tm, S_COLS), lambda b, i: (i, b))],
        out_shape=[jax.ShapeDtypeStruct((bsz, tp, PM_COLS), F32),
                   jax.ShapeDtypeStruct((bsz, tp, R_COLS), F32),
                   jax.ShapeDtypeStruct((bsz, tp, H_COLS), F32),
                   jax.ShapeDtypeStruct((tp, bsz * S_COLS), F32)],
        compiler_params=_params(("parallel", "parallel")),
        name="proj_in",
    )(h, g, w)


def _mlstm_kernel(pm_ref, cw_ref, cb_ref, gb_ref, ng_ref, y_ref, c_sc, n_sc, m_sc, halo_sc, *, tb):
    blk = pl.program_id(1)

    @pl.when(blk == 0)
    def _():
        c_sc[...] = jnp.zeros_like(c_sc)
        n_sc[...] = jnp.zeros_like(n_sc)
        m_sc[...] = jnp.zeros_like(m_sc)
        halo_sc[...] = jnp.zeros_like(halo_sc)

    bd, bones, t_io, s_io, tri = _head_consts()
    causal = t_io >= s_io
    eye4 = (t_io == s_io).astype(F32)
    ones64 = jnp.ones((CHUNK, CHUNK), BF16)
    head_of_lane = _iota((CHUNK, W_MIX), 1) >> 6
    cw = cw_ref[...]
    cb = cb_ref[...]
    gb = gb_ref[...]
    ng = ng_ref[...]

    def chunk(c, carry):
        r0 = pl.multiple_of(c * CHUNK, CHUNK)
        cur = pm_ref[pl.ds(r0, CHUNK), :]
        prev = pm_ref[pl.ds(pl.multiple_of(jnp.maximum(r0 - 8, 0), 8), 8), 0:2 * W_MIX]
        prev = jnp.where(c == 0, halo_sc[...], prev)
        ext = jnp.concatenate([prev, cur[:, 0:2 * W_MIX]], axis=0)
        acc = cb + cw[0:1] * ext[5:5 + CHUNK]
        for j in range(1, MLSTM_CONV):
            acc = acc + cw[j:j + 1] * ext[5 + j:5 + j + CHUNK]
        valid = (blk * tb + r0 + _iota((CHUNK, 1), 0)) >= PAD_FRONT
        qk = jnp.where(valid, _silu(acc), 0.0)
        q = qk[:, 0:W_MIX] * (HEAD_DIM ** -0.5)
        k = qk[:, W_MIX:2 * W_MIX]
        v = cur[:, 2 * W_MIX:3 * W_MIX]
        o = cur[:, 3 * W_MIX:4 * W_MIX]
        li = jnp.where(valid, cur[:, 4 * W_MIX:5 * W_MIX] + gb[0:1], NEG)
        lf = jnp.where(valid, _log_sigmoid(cur[:, 5 * W_MIX:6 * W_MIX] + gb[1:2]), 0.0)

        c_st = c_sc[...]
        n_st = n_sc[...]
        m_st = m_sc[...]

        b = _dot01(tri, lf)
        b_row = _dot01(ones64, b * eye4)
        li_row = _dot01(ones64, li * eye4)
        dmat = jnp.where(causal, b - b_row + li_row, NEG)
        mx = jnp.full((CHUNK, W_MIX), -3.0e38, F32)
        for h in range(N_HEADS):
            sel = head_of_lane == h
            mh = jnp.max(jnp.where(sel, dmat, -3.0e38), axis=-1, keepdims=True)
            mx = jnp.where(sel, mh, mx)
        inter = b + m_st
        m = jnp.maximum(inter, mx)
        wmat = jnp.exp(dmat - m)
        sc = jnp.exp(inter - m)
        s = _dot_nt(q, _tile_heads(k, bd)) * wmat
        sb = s.astype(BF16)
        num = sc * _dot(q, c_st) + _dot(sb, _tile_heads(v, bd))
        den = sc * _dot(q * n_st, bones) + _dot(sb, bones)
        hh = num / jnp.maximum(jnp.abs(den), jnp.exp(-m))

        b_last = b[CHUNK - 1:CHUNK]
        g = b_last - b + li
        m_new = jnp.maximum(b_last + m_st, jnp.max(g, axis=0, keepdims=True))
        ws = jnp.exp(g - m_new)
        dec = jnp.exp(b_last + m_st - m_new)
        kw = k * ws
        c_sc[...] = dec * c_st + jnp.where(bd, _dot_tn(kw, v), 0.0)
        n_sc[...] = dec * n_st + jnp.sum(kw, axis=0, keepdims=True)
        m_sc[...] = m_new

        y = hh * lax.rsqrt(_head_mean(hh * hh, bones) + RMS_EPS) * ng
        y_ref[pl.ds(r0, CHUNK), :] = _sigmoid(o) * y
        return carry

    lax.fori_loop(0, tb // CHUNK, chunk, 0)
    halo_sc[...] = pm_ref[tb - 8:tb, 0:2 * W_MIX]


def _mlstm(pm, conv_w, conv_b, gate_b, norm_g, tb):
    bsz, tp, _ = pm.shape
    return pl.pallas_call(
        functools.partial(_mlstm_kernel, tb=tb),
        grid=(bsz, tp // tb),
        in_specs=[pl.BlockSpec((None, tb, PM_COLS), lambda b, i: (b, i, 0)),
                  _const_spec((MLSTM_CONV, 2 * W_MIX)), _const_spec((1, 2 * W_MIX)),
                  _const_spec((2, W_MIX)), _const_spec((1, W_MIX))],
        out_specs=pl.BlockSpec((None, tb, W_MIX), lambda b, i: (b, i, 0)),
        out_shape=jax.ShapeDtypeStruct((bsz, tp, W_MIX), F32),
        scratch_shapes=[pltpu.VMEM((W_MIX, W_MIX), F32), pltpu.VMEM((1, W_MIX), F32),
                        pltpu.VMEM((1, W_MIX), F32), pltpu.VMEM((8, 2 * W_MIX), F32)],
        compiler_params=_params(("parallel", "arbitrary")),
        name="mlstm",
    )(pm, conv_w, conv_b, gate_b, norm_g)


def _hgrn2_kernel(ph_ref, lb_ref, ng_ref, y_ref, s_sc, *, tb):
    blk = pl.program_id(1)

    @pl.when(blk == 0)
    def _():
        s_sc[...] = jnp.zeros_like(s_sc)

    bd, bones, _, _, tri = _head_consts()
    t16 = _iota((SUB, W_MIX), 0)
    lb = lb_ref[...]
    ng = ng_ref[...]
    log_lb = jnp.log(jnp.maximum(lb, LB_FLOOR))
    log1m_lb = jnp.log1p(-lb)

    def chunk(c, carry):
        r0 = pl.multiple_of(c * CHUNK, CHUNK)
        cur = ph_ref[pl.ds(r0, CHUNK), :]
        valid = (blk * tb + r0 + _iota((CHUNK, 1), 0)) >= PAD_FRONT
        z = cur[:, W_MIX:2 * W_MIX]
        q = jnp.where(valid, _silu(cur[:, 0:W_MIX]), 0.0)
        k = jnp.where(valid, (1.0 - lb) * _sigmoid(-z), 0.0)
        v = cur[:, 2 * W_MIX:3 * W_MIX]
        gate = cur[:, 3 * W_MIX:4 * W_MIX]
        bb = log1m_lb + _log_sigmoid(z)
        lf = jnp.maximum(log_lb, bb) + jnp.log1p(jnp.exp(-jnp.abs(log_lb - bb)))
        lf = jnp.where(valid, lf, 0.0)
        g = _dot01(tri, lf)

        st = s_sc[...]
        outs = []
        for i in range(CHUNK // SUB):
            lo = i * SUB
            gl = g[lo:lo + SUB] if i == 0 else g[lo:lo + SUB] - g[lo - 1:lo]
            qi, ki, vi = q[lo:lo + SUB], k[lo:lo + SUB], v[lo:lo + SUB]
            g_end = gl[SUB - 1:SUB]
            o = _dot_nt(qi * jnp.exp(gl), st)
            rows = []
            for s in range(SUB):
                d = jnp.where(t16 >= s, gl - gl[s:s + 1], NEG)
                rows.append(qi * ki[s:s + 1] * jnp.exp(d))
            att = _dot(jnp.concatenate(rows, axis=0), bones)
            for s in range(SUB):
                o = o + att[s * SUB:(s + 1) * SUB] * vi[s:s + 1]
            outs.append(o)
            kt = ki * jnp.exp(g_end - gl)
            st = jnp.exp(g_end) * st + jnp.where(bd, _dot_tn(vi, kt), 0.0)
        s_sc[...] = st
        o = jnp.concatenate(outs, axis=0)
        y = o * lax.rsqrt(_head_mean(o * o, bones) + RMS_EPS) * ng
        y_ref[pl.ds(r0, CHUNK), :] = y * _silu(gate)
        return carry

    lax.fori_loop(0, tb // CHUNK, chunk, 0)


def _hgrn2(ph, lb, norm_g, tb):
    bsz, tp, _ = ph.shape
    return pl.pallas_call(
        functools.partial(_hgrn2_kernel, tb=tb),
        grid=(bsz, tp // tb),
        in_specs=[pl.BlockSpec((None, tb, H_COLS), lambda b, i: (b, i, 0)),
                  _const_spec((1, W_MIX)), _const_spec((1, W_MIX))],
        out_specs=pl.BlockSpec((None, tb, W_MIX), lambda b, i: (b, i, 0)),
        out_shape=jax.ShapeDtypeStruct((bsz, tp, W_MIX), F32),
        scratch_shapes=[pltpu.VMEM((W_MIX, W_MIX), F32)],
        compiler_params=_params(("parallel", "arbitrary")),
        name="hgrn2",
    )(ph, lb, norm_g)


def _rwkv7_kernel(*refs, tb, first):
    if first:
        (pr_ref, mu_ref, wc_ref, vec_ref, y_ref, vf_out_ref, s_sc, halo_sc) = refs
    else:
        (pr_ref, vf_ref, mu_ref, wc_ref, vec_ref, v1_ref, v2_ref, y_ref, s_sc, halo_sc) = refs
    blk = pl.program_id(1)

    @pl.when(blk == 0)
    def _():
        s_sc[...] = jnp.zeros_like(s_sc)
        halo_sc[...] = jnp.zeros_like(halo_sc)

    bd, bones, t_io, s_io, tri = _head_consts()
    causal = t_io >= s_io
    strict = t_io > s_io
    eye4 = (t_io == s_io).astype(F32)
    lane = _iota((CHUNK, W_MIX), 1)
    mu = mu_ref[...]
    vec = vec_ref[...]
    w0, a0, k_k, k_a, r_k, gn_w, gn_b, v0 = [vec[i:i + 1] for i in range(8)]

    def hmm(x, y):
        return _dot(x, _tile_heads(y, bd))

    def chunk(c, carry):
        r0 = pl.multiple_of(c * CHUNK, CHUNK)
        cur = pr_ref[pl.ds(r0, CHUNK), :]
        prev = pr_ref[pl.ds(pl.multiple_of(jnp.maximum(r0 - 8, 0), 8), 8), :]
        prev = jnp.where(c == 0, halo_sc[...], prev)
        shifted = jnp.concatenate([prev, cur], axis=0)[7:7 + CHUNK]
        xm = cur + (shifted - cur) * mu
        r = xm[:, 0:W_MIX]
        k = xm[:, W_MIX:2 * W_MIX]
        v = xm[:, 2 * W_MIX:3 * W_MIX]
        cc = xm[:, 3 * W_MIX:4 * W_MIX]
        feat = jnp.where(lane < RANK_W, jnp.tanh(cc),
                         jnp.where(lane < RANK_W + RANK_A, cc, _sigmoid(cc)))
        proj = _dot(feat, wc_ref[...])
        w_raw = -_softplus(-(w0 + proj[:, 0:W_MIX])) - 0.5
        a_gate = _sigmoid(a0 + proj[:, W_MIX:2 * W_MIX])
        g_out = proj[:, 2 * W_MIX:3 * W_MIX]
        if first:
            vf_out_ref[pl.ds(r0, CHUNK), :] = v
        else:
            vf = vf_ref[pl.ds(r0, CHUNK), :]
            mix = _sigmoid(v0 + _dot(_dot(v, v1_ref[...]), v2_ref[...]))
            v = v + (vf - v) * mix
        kk = k * k_k
        kk = kk / jnp.maximum(jnp.sqrt(_dot(kk * kk, bones)), L2_EPS)
        k = k * (1.0 + (a_gate - 1.0) * k_a)
        lw = -jnp.exp(w_raw)
        a = -kk
        b = kk * a_gate

        cum = _dot01(tri, lw)
        tot = cum[CHUNK - 1:CHUNK]
        e_neg = jnp.exp(-cum)
        e_end = jnp.exp(tot - cum)
        a_t = a * jnp.exp(cum - lw)
        r_t = r * jnp.exp(cum)
        ar = jnp.concatenate([a_t, r_t], axis=0)
        nb = _dot_nt(ar, _tile_heads(b * e_neg, bd))
        nk = _dot_nt(ar, _tile_heads(k * e_neg, bd))
        n_ab = jnp.where(strict, nb[0:CHUNK], 0.0)
        m_rb = jnp.where(causal, nb[CHUNK:], 0.0)
        n_ak = jnp.where(strict, nk[0:CHUNK], 0.0)
        m_rk = jnp.where(causal, nk[CHUNK:], 0.0)

        p = n_ab
        tinv = eye4 + p
        for _ in range(5):
            p = hmm(p, p)
            tinv = hmm(tinv, eye4 + p)

        s_st = s_sc[...]
        a_hat = hmm(tinv, a_t)
        v_hat = hmm(tinv, hmm(n_ak, v))
        u = _dot_nt(a_hat, s_st) + v_hat
        y = _dot_nt(r_t, s_st) + hmm(m_rb, u) + hmm(m_rk, v)
        s_sc[...] = s_st * jnp.exp(tot) + jnp.where(
            bd, _dot_tn(u, b * e_end) + _dot_tn(v, k * e_end), 0.0)

        mean = _head_mean(y, bones)
        yc = y - mean
        var = _head_mean(yc * yc, bones)
        yn = yc * lax.rsqrt(var + GN_EPS) * gn_w + gn_b
        bonus = _dot(r * k * r_k, bones)
        y_ref[pl.ds(r0, CHUNK), :] = (yn + bonus * v) * g_out
        return carry

    lax.fori_loop(0, tb // CHUNK, chunk, 0)
    halo_sc[...] = pr_ref[tb - 8:tb, :]


def _rwkv7(pr, v_first, mu, wc, vec, v1, v2, tb):
    bsz, tp, _ = pr.shape
    first = v_first is None
    blk = lambda n: pl.BlockSpec((None, tb, n), lambda b, i: (b, i, 0))
    yshape = jax.ShapeDtypeStruct((bsz, tp, W_MIX), F32)
    common = [_const_spec((1, R_COLS)), _const_spec((W_MIX, 3 * W_MIX)), _const_spec((8, W_MIX))]
    if first:
        in_specs = [blk(R_COLS)] + common
        args = (pr, mu, wc, vec)
        out_specs, out_shape = [blk(W_MIX), blk(W_MIX)], [yshape, yshape]
    else:
        in_specs = [blk(R_COLS), blk(W_MIX)] + common + [_const_spec((W_MIX, RANK_V)),
                                                          _const_spec((RANK_V, W_MIX))]
        args = (pr, v_first, mu, wc, vec, v1, v2)
        out_specs, out_shape = blk(W_MIX), yshape
    out = pl.pallas_call(
        functools.partial(_rwkv7_kernel, tb=tb, first=first),
        grid=(bsz, tp // tb),
        in_specs=in_specs, out_specs=out_specs, out_shape=out_shape,
        scratch_shapes=[pltpu.VMEM((W_MIX, W_MIX), F32), pltpu.VMEM((8, R_COLS), F32)],
        compiler_params=_params(("parallel", "arbitrary")),
        name="rwkv7_first" if first else "rwkv7",
    )(*args)
    return (out[0], out[1]) if first else (out, v_first)


def _s5_disc_kernel(are_ref, aim_ref, ls_ref, bre_ref, bim_ref, ar_ref, ai_ref, bbre_ref, bbim_ref):
    a_re = are_ref[...]
    a_im = aim_ref[...]
    step = jnp.exp(ls_ref[...])
    mag = jnp.exp(a_re * step)
    ang = a_im * step
    ab_re = mag * jnp.cos(ang)
    ab_im = mag * jnp.sin(ang)
    den = a_re * a_re + a_im * a_im
    num_re = ab_re - 1.0
    coef_re = (num_re * a_re + ab_im * a_im) / den
    coef_im = (ab_im * a_re - num_re * a_im) / den
    ar_ref[...] = ab_re
    ai_ref[...] = ab_im
    b_re = bre_ref[...]
    b_im = bim_ref[...]
    bbre_ref[...] = coef_re[None] * b_re - coef_im[None] * b_im
    bbim_ref[...] = coef_re[None] * b_im + coef_im[None] * b_re


def _s5_discretize(a_re, a_im, log_step, b_re_t, b_im_t):
    nl = a_re.shape[0]
    gp = pl.BlockSpec((None, S5_GROUPS, S5_STATE), lambda l: (l, 0, 0))
    cgp = pl.BlockSpec((None, S5_GROUP, S5_GROUPS, S5_STATE), lambda l: (l, 0, 0, 0))
    return pl.pallas_call(
        _s5_disc_kernel,
        grid=(nl,),
        in_specs=[gp, gp, pl.BlockSpec((None, S5_GROUPS, 1), lambda l: (l, 0, 0)), cgp, cgp],
        out_specs=[gp, gp, cgp, cgp],
        out_shape=[jax.ShapeDtypeStruct(a_re.shape, F32), jax.ShapeDtypeStruct(a_re.shape, F32),
                   jax.ShapeDtypeStruct(b_re_t.shape, F32), jax.ShapeDtypeStruct(b_re_t.shape, F32)],
        compiler_params=_params(("parallel",)),
        name="s5_discretize",
    )(a_re, a_im, log_step, b_re_t, b_im_t)


def _s5_kernel(u_ref, wbu_ref, ar_ref, ai_ref, wcre_ref, wcim_ref, d_ref, wglu_ref, bglu_ref,
               y_ref, x_sc, st_sc, *, tb, nb):
    @pl.when(pl.program_id(0) == 0)
    def _():
        st_sc[...] = jnp.zeros_like(st_sc)

    u = u_ref[...]
    x_sc[...] = _dot(u, wbu_ref[...])
    ar = jnp.broadcast_to(ar_ref[...], (nb, S5_N))
    ai = jnp.broadcast_to(ai_ref[...], (nb, S5_N))

    def step(t, carry):
        xr, xi = carry
        rows = pl.ds(pl.multiple_of(t * nb, nb), nb)
        nr = ar * xr - ai * xi + x_sc[rows, 0:S5_N]
        ni = ar * xi + ai * xr + x_sc[rows, S5_N:2 * S5_N]
        x_sc[rows, 0:S5_N] = nr
        x_sc[rows, S5_N:2 * S5_N] = ni
        return nr, ni

    xr, xi = lax.fori_loop(0, tb, step, (st_sc[:, 0:S5_N], st_sc[:, S5_N:2 * S5_N]), unroll=8)
    st_sc[:, 0:S5_N] = xr
    st_sc[:, S5_N:2 * S5_N] = xi
    y = _dot(x_sc[:, 0:S5_N], wcre_ref[...]) - _dot(x_sc[:, S5_N:2 * S5_N], wcim_ref[...])
    y = _gelu_tanh(y + d_ref[...] * u)
    y_ref[...] = y * _sigmoid(_dot(y, wglu_ref[...]) + bglu_ref[...])


def _s5(u_tb, wbu, ar, ai, wcre, wcim, d_skip, wglu, bglu, nb, tb):
    rows = u_tb.shape[0]
    tp = rows // nb
    return pl.pallas_call(
        functools.partial(_s5_kernel, tb=tb, nb=nb),
        grid=(tp // tb,),
        in_specs=[pl.BlockSpec((tb * nb, W_MIX), lambda i: (i, 0)),
                  _const_spec((W_MIX, 2 * S5_N)), _const_spec((1, S5_N)), _const_spec((1, S5_N)),
                  _const_spec((S5_N, W_MIX)), _const_spec((S5_N, W_MIX)), _const_spec((1, W_MIX)),
                  _const_spec((W_MIX, W_MIX)), _const_spec((1, W_MIX))],
        out_specs=pl.BlockSpec((tb * nb, W_MIX), lambda i: (i, 0)),
        out_shape=jax.ShapeDtypeStruct((rows, W_MIX), F32),
        scratch_shapes=[pltpu.VMEM((tb * nb, 2 * S5_N), F32), pltpu.VMEM((nb, 2 * S5_N), F32)],
        compiler_params=_params(("arbitrary",)),
        name="s5",
    )(u_tb, wbu, ar, ai, wcre, wcim, d_skip, wglu, bglu)


def _merge_kernel(x_ref, ym_ref, yr_ref, yh_ref, ys_ref, g0_ref, g1_ref, wg_ref, bg_ref, wbr_ref,
                  wo_ref, o_ref):
    x = x_ref[...]
    ub = _rmsnorm_rows(x, g0_ref[...]).astype(BF16)
    merged = None
    for n, y_ref in enumerate((ym_ref, yr_ref, yh_ref, ys_ref)):
        gate = _sigmoid(jnp.dot(ub, wg_ref[n], preferred_element_type=F32) + bg_ref[n])
        term = gate * _dot(y_ref[...], wbr_ref[n])
        merged = term if merged is None else merged + term
    o_ref[...] = x + _rmsnorm_rows(_dot(merged, wo_ref[...]), g1_ref[...])


def _merge(h, ym, yr, yh, ys_tm, g0, g1, wg, bg, wbr, wo, tm):
    bsz, tp, d = h.shape
    row = lambda n: pl.BlockSpec((None, tm, n), lambda b, i: (b, i, 0))
    return pl.pallas_call(
        _merge_kernel,
        grid=(bsz, tp // tm),
        in_specs=[row(d), row(W_MIX), row(W_MIX), row(W_MIX),
                  pl.BlockSpec((tm, W_MIX), lambda b, i: (i, b)),
                  _const_spec((1, d)), _const_spec((1, d)),
                  _const_spec((N_BRANCH, d, d)), _const_spec((N_BRANCH, 1, d)),
                  _const_spec((N_BRANCH, W_MIX, d)), _const_spec((d, d))],
        out_specs=row(d),
        out_shape=jax.ShapeDtypeStruct(h.shape, F32),
        compiler_params=_params(("parallel", "parallel")),
        name="merge",
    )(h, ym, yr, yh, ys_tm, g0, g1, wg, bg, wbr, wo)


FFN_COLS = 256


def _ffn_kernel(x_ref, g2_ref, g3_ref, wup_ref, cw_ref, cb_ref, wdn_ref, o_ref, halo_sc, *, tm):
    i = pl.program_id(1)

    @pl.when(i == 0)
    def _():
        halo_sc[...] = jnp.zeros_like(halo_sc)

    x = x_ref[...]
    ub = _rmsnorm_rows(x, g2_ref[...]).astype(BF16)
    valid = (i * tm + _iota((tm, 1), 0)) >= PAD_FRONT

    def conv(c0):
        z = jnp.dot(ub, wup_ref[:, c0:c0 + FFN_COLS], preferred_element_type=F32)
        z = jnp.where(valid, z, 0.0)
        ext = jnp.concatenate([halo_sc[:, c0:c0 + FFN_COLS], z], axis=0)
        halo_sc[:, c0:c0 + FFN_COLS] = z[tm - 8:tm]
        w = cw_ref[:, c0:c0 + FFN_COLS]
        return (cb_ref[:, c0:c0 + FFN_COLS] + w[2:3] * z + w[1:2] * ext[7:7 + tm]
                + w[0:1] * ext[6:6 + tm])

    acc = None
    for j in range(D_FF // FFN_COLS):
        a = _gelu_tanh(conv(j * FFN_COLS)) * conv(D_FF + j * FFN_COLS)
        part = _dot(a, wdn_ref[j * FFN_COLS:(j + 1) * FFN_COLS, :])
        acc = part if acc is None else acc + part
    o_ref[...] = x + _rmsnorm_rows(acc, g3_ref[...])


def _ffn(h, g2, g3, wup, conv_w, conv_b, wdn, tm):
    bsz, tp, d = h.shape
    row = pl.BlockSpec((None, tm, d), lambda b, i: (b, i, 0))
    return pl.pallas_call(
        functools.partial(_ffn_kernel, tm=tm),
        grid=(bsz, tp // tm),
        in_specs=[row, _const_spec((1, d)), _const_spec((1, d)), _const_spec((d, 2 * D_FF)),
                  _const_spec((FFN_CONV, 2 * D_FF)), _const_spec((1, 2 * D_FF)),
                  _const_spec((D_FF, d))],
        out_specs=row,
        out_shape=jax.ShapeDtypeStruct(h.shape, F32),
        scratch_shapes=[pltpu.VMEM((8, 2 * D_FF), F32)],
        compiler_params=_params(("parallel", "arbitrary")),
        name="conv_ffn",
    )(h, g2, g3, wup, conv_w, conv_b, wdn)


def _block_diag_in(bb):
    eye = jnp.eye(S5_GROUPS, dtype=F32)
    return jnp.einsum('cgp,gh->gchp', bb, eye).reshape(W_MIX, S5_N)


def _block_diag_out(c):
    eye = jnp.eye(S5_GROUPS, dtype=F32)
    return jnp.einsum('gcp,gh->gphc', c, eye).reshape(S5_N, W_MIX)


def kernel(x, meta, norms, w_in, w_gate, b_gate, w_branch, w_out, m_conv_w, m_conv_b, m_gate_b, m_norm,
           r_mu, r_w0, r_w2, r_a0, r_a2, r_g2, r_kk, r_ka, r_rk, r_gn_w, r_gn_b, r_v0, r_v1, r_v2,
           h_lb, h_norm, s_a_re, s_a_im, s_b_re, s_b_im, s_c_re, s_c_im, s_d, s_log_step, s_w_glu, s_b_glu,
           f_up, f_conv_w, f_conv_b, f_down):
    bsz, seq, d = x.shape
    depth = w_in.shape[0]
    t_all = N_META + seq
    tp = PAD_FRONT + t_all
    assert d == D_MODEL and tp % CHUNK == 0
    tm = _row_tile(tp, 688)
    tb = _chunk_block(tp, 192)
    ts5 = _chunk_block(tp, 192)

    h = jnp.concatenate([jnp.zeros((bsz, PAD_FRONT, d), F32),
                         jnp.broadcast_to(meta.astype(F32)[None], (bsz, N_META, d)), x], axis=1)

    lb_w = jax.nn.softmax(h_lb.astype(F32), axis=0)
    lbs = jnp.cumsum(lb_w, axis=0) - lb_w[0:1]

    ar, ai, bbre, bbim = _s5_discretize(
        s_a_re, s_a_im, s_log_step[..., None],
        jnp.transpose(s_b_re, (0, 3, 1, 2)), jnp.transpose(s_b_im, (0, 3, 1, 2)))

    row = lambda a: a.reshape(1, -1)
    v_first = None
    for l in range(depth):
        w = w_in[l]
        w_r = jnp.concatenate(
            [w[:, 0:4 * W_MIX],
             jnp.repeat(w[:, 4 * W_MIX:4 * W_MIX + N_HEADS], HEAD_DIM, axis=1),
             jnp.repeat(w[:, 4 * W_MIX + N_HEADS:M_COLS], HEAD_DIM, axis=1),
             w[:, M_COLS:]], axis=1).astype(BF16)
        pm, pr, ph, ps_tm = _proj_in(h, row(norms[l, 0]), w_r, tm)

        ym = _mlstm(pm, m_conv_w[l], row(m_conv_b[l]), jnp.repeat(m_gate_b[l], HEAD_DIM, axis=1),
                    row(m_norm[l]), tb)

        wc = jnp.zeros((W_MIX, 3 * W_MIX), F32)
        wc = wc.at[0:RANK_W, 0:W_MIX].set(r_w2[l])
        wc = wc.at[RANK_W:RANK_W + RANK_A, W_MIX:2 * W_MIX].set(r_a2[l])
        wc = wc.at[RANK_W + RANK_A:, 2 * W_MIX:].set(r_g2[l]).astype(BF16)
        v0 = r_v0[l - 1] if l > 0 else jnp.zeros((W_MIX,), F32)
        vec = jnp.stack([r_w0[l], r_a0[l], r_kk[l], r_ka[l], r_rk[l], r_gn_w[l], r_gn_b[l], v0])
        if l == 0:
            yr, v_first = _rwkv7(pr, None, row(r_mu[l]), wc, vec, None, None, tb)
        else:
            yr, _ = _rwkv7(pr, v_first, row(r_mu[l]), wc, vec, r_v1[l - 1].astype(BF16),
                           r_v2[l - 1].astype(BF16), tb)

        yh = _hgrn2(ph, row(lbs[l]), row(h_norm[l]), tb)

        wbu = jnp.concatenate([_block_diag_in(bbre[l]), _block_diag_in(bbim[l])], axis=1).astype(BF16)
        ys_tb = _s5(ps_tm.reshape(tp * bsz, W_MIX), wbu, row(ar[l]), row(ai[l]),
                    _block_diag_out(s_c_re[l]).astype(BF16), _block_diag_out(s_c_im[l]).astype(BF16),
                    row(s_d[l]), s_w_glu[l].astype(BF16), row(s_b_glu[l]), bsz, ts5)

        h = _merge(h, ym, yr, yh, ys_tb.reshape(tp, bsz * W_MIX), row(norms[l, 0]), row(norms[l, 1]),
                   w_gate[l].astype(BF16), b_gate[l][:, None, :], w_branch[l].astype(BF16),
                   w_out[l].astype(BF16), tm)
        h = _ffn(h, row(norms[l, 2]), row(norms[l, 3]), f_up[l].astype(BF16), f_conv_w[l],
                 row(f_conv_b[l]), f_down[l].astype(BF16), tm)
    return h[:, PAD_FRONT + N_META:]
```

```python
import functools
import math

import jax
import jax.numpy as jnp
from jax import lax
from jax.experimental import pallas as pl
from jax.experimental.pallas import tpu as pltpu

F32 = jnp.float32
BF16 = jnp.bfloat16

D_MODEL = 1024
N_META = 16
CHUNK = 64
PAD_FRONT = CHUNK - N_META
N_BRANCH = 4
W_MIX = D_MODEL // 4
HEAD_DIM = 64
N_HEADS = W_MIX // HEAD_DIM
MLSTM_CONV = 4
RANK_W = 64
RANK_A = 64
RANK_V = 32
RANK_G = 128
S5_GROUP = 16
S5_GROUPS = W_MIX // S5_GROUP
S5_STATE = 64
S5_N = S5_GROUPS * S5_STATE
D_FF = 256 * ((8 * D_MODEL // 3 + 255) // 256)
FFN_CONV = 3
M_COLS = 4 * W_MIX + 2 * N_HEADS
R_COLS = 3 * W_MIX + RANK_W + RANK_A + RANK_G
H_COLS = 4 * W_MIX
S_COLS = W_MIX
PM_COLS = 6 * W_MIX
RMS_EPS = 1e-6
GN_EPS = 64e-5
L2_EPS = 1e-12
LB_FLOOR = 1e-30
NEG = -1e30
SUB = 16
NSEQ = 4

VMEM_LIMIT_BYTES = 56 * 1024 * 1024


def _dot(a, b):
    return jnp.dot(a.astype(BF16), b.astype(BF16), preferred_element_type=F32)


def _dot_nt(a, b):
    return lax.dot_general(a.astype(BF16), b.astype(BF16), (((1,), (1,)), ((), ())),
                           preferred_element_type=F32)


def _dot_tn(a, b):
    return lax.dot_general(a.astype(BF16), b.astype(BF16), (((0,), (0,)), ((), ())),
                           preferred_element_type=F32)


def _dot01(sel, x):
    hi = x.astype(BF16)
    r1 = x - hi.astype(F32)
    mid = r1.astype(BF16)
    lo = (r1 - mid.astype(F32)).astype(BF16)
    out = jnp.dot(sel, hi, preferred_element_type=F32)
    out = out + jnp.dot(sel, mid, preferred_element_type=F32)
    return out + jnp.dot(sel, lo, preferred_element_type=F32)


def _sigmoid(x):
    return 1.0 / (1.0 + jnp.exp(-x))


def _softplus(x):
    return jnp.maximum(x, 0.0) + jnp.log(1.0 + jnp.exp(-jnp.abs(x)))


def _log_sigmoid(x):
    return -_softplus(-x)


def _silu(x):
    return x * _sigmoid(x)


def _gelu_tanh(x):
    c = math.sqrt(2.0 / math.pi)
    return 0.5 * x * (1.0 + jnp.tanh(c * (x + 0.044715 * (x * x * x))))


def _rmsnorm_rows(x, g):
    ms = jnp.mean(x * x, axis=-1, keepdims=True)
    return x * lax.rsqrt(ms + RMS_EPS) * g


def _iota(shape, dim):
    return lax.broadcasted_iota(jnp.int32, shape, dim)


def _head_consts():
    r = _iota((W_MIX, W_MIX), 0)
    c = _iota((W_MIX, W_MIX), 1)
    bd = (r >> 6) == (c >> 6)
    t = _iota((CHUNK, W_MIX), 0)
    s = _iota((CHUNK, W_MIX), 1) & (HEAD_DIM - 1)
    tri = (_iota((CHUNK, CHUNK), 0) >= _iota((CHUNK, CHUNK), 1)).astype(BF16)
    return bd, bd.astype(BF16), t, s, tri


def _tile_heads(x, bd):
    xb = x.astype(BF16)
    return jnp.where(bd, jnp.concatenate([xb, xb, xb, xb], axis=0), jnp.zeros((), BF16))


def _head_mean(x, bones):
    return _dot(x, bones) * (1.0 / HEAD_DIM)


def _row_tile(tp, target):
    best = None
    for cand in range(16, tp + 1, 16):
        if tp % cand == 0 and cand <= target:
            best = cand
    return best if best is not None else tp


def _chunk_block(tp, target):
    best = CHUNK
    for cand in range(CHUNK, tp + 1, CHUNK):
        if tp % cand == 0 and cand <= target:
            best = cand
    return best


def _const_spec(shape):
    nd = len(shape)
    return pl.BlockSpec(shape, lambda *_: (0,) * nd)


def _round_robin(gens):
    gens = list(gens)
    while gens:
        alive = []
        for g in gens:
            try:
                next(g)
                alive.append(g)
            except StopIteration:
                pass
        gens = alive


def _params(sem):
    return pltpu.CompilerParams(dimension_semantics=sem, vmem_limit_bytes=VMEM_LIMIT_BYTES)


def _proj_in_kernel(x_ref, g_ref, w_ref, pm_ref, pr_ref, ph_ref, ps_ref, *, tm):
    i = pl.program_id(1)
    x = x_ref[...]
    u = _rmsnorm_rows(x, g_ref[...])
    rows = i * tm + _iota((tm, 1), 0)
    ub = jnp.where(rows >= PAD_FRONT, u, 0.0).astype(BF16)
    c0 = 0
    for ref, n in ((pm_ref, PM_COLS), (pr_ref, R_COLS), (ph_ref, H_COLS), (ps_ref, S_COLS)):
        ref[...] = jnp.dot(ub, w_ref[:, c0:c0 + n], preferred_element_type=F32)
        c0 += n


def _proj_in(h, g, w, tm):
    bsz, tp, d = h.shape
    ncols = w.shape[1]
    row = lambda n: pl.BlockSpec((None, tm, n), lambda b, i: (b, i, 0))
    return pl.pallas_call(
        functools.partial(_proj_in_kernel, tm=tm),
        grid=(bsz, tp // tm),
        in_specs=[row(d), _const_spec((1, d)), _const_spec((d, ncols))],
        out_specs=[row(PM_COLS), row(R_COLS), row(H_COLS),
                   pl.BlockSpec((tm, S_COLS), lambda b, i: (i, b))],
        out_shape=[jax.ShapeDtypeStruct((bsz, tp, PM_COLS), F32),
                   jax.ShapeDtypeStruct((bsz, tp, R_COLS), F32),
                   jax.ShapeDtypeStruct((bsz, tp, H_COLS), F32),
                   jax.ShapeDtypeStruct((tp, bsz * S_COLS), F32)],
        compiler_params=_params(("parallel", "parallel")),
        name="proj_in",
    )(h, g, w)


def _mlstm_kernel(pm_ref, cw_ref, cb_ref, gb_ref, ng_ref, y_ref, c_sc, n_sc, m_sc, halo_sc, *, tb, nseq):
    blk = pl.program_id(1)

    @pl.when(blk == 0)
    def _():
        c_sc[...] = jnp.zeros_like(c_sc)
        n_sc[...] = jnp.zeros_like(n_sc)
        m_sc[...] = jnp.zeros_like(m_sc)
        halo_sc[...] = jnp.zeros_like(halo_sc)

    bd, bones, t_io, s_io, tri = _head_consts()
    causal = t_io >= s_io
    eye4 = (t_io == s_io).astype(F32)
    ones64 = jnp.ones((CHUNK, CHUNK), BF16)
    head_of_lane = _iota((CHUNK, W_MIX), 1) >> 6
    cw = cw_ref[...]
    cb = cb_ref[...]
    gb = gb_ref[...]
    ng = ng_ref[...]

    def chunk_seq(c, r0, sq):
        cur = pm_ref[sq, pl.ds(r0, CHUNK), :]
        prev = pm_ref[sq, pl.ds(pl.multiple_of(jnp.maximum(r0 - 8, 0), 8), 8), 0:2 * W_MIX]
        prev = jnp.where(c == 0, halo_sc[sq], prev)
        ext = jnp.concatenate([prev, cur[:, 0:2 * W_MIX]], axis=0)
        acc = cb + cw[0:1] * ext[5:5 + CHUNK]
        for j in range(1, MLSTM_CONV):
            acc = acc + cw[j:j + 1] * ext[5 + j:5 + j + CHUNK]
        valid = (blk * tb + r0 + _iota((CHUNK, 1), 0)) >= PAD_FRONT
        qk = jnp.where(valid, _silu(acc), 0.0)
        q = qk[:, 0:W_MIX] * (HEAD_DIM ** -0.5)
        k = qk[:, W_MIX:2 * W_MIX]
        v = cur[:, 2 * W_MIX:3 * W_MIX]
        o = cur[:, 3 * W_MIX:4 * W_MIX]
        li = jnp.where(valid, cur[:, 4 * W_MIX:5 * W_MIX] + gb[0:1], NEG)
        lf = jnp.where(valid, _log_sigmoid(cur[:, 5 * W_MIX:6 * W_MIX] + gb[1:2]), 0.0)

        c_st = c_sc[sq]
        n_st = n_sc[sq]
        m_st = m_sc[sq]

        b = _dot01(tri, lf)
        yield
        b_row = _dot01(ones64, b * eye4)
        li_row = _dot01(ones64, li * eye4)
        yield
        dmat = jnp.where(causal, b - b_row + li_row, NEG)
        mx = jnp.full((CHUNK, W_MIX), -3.0e38, F32)
        for h in range(N_HEADS):
            sel = head_of_lane == h
            mh = jnp.max(jnp.where(sel, dmat, -3.0e38), axis=-1, keepdims=True)
            mx = jnp.where(sel, mh, mx)
        inter = b + m_st
        m = jnp.maximum(inter, mx)
        wmat = jnp.exp(dmat - m)
        sc = jnp.exp(inter - m)
        s = _dot_nt(q, _tile_heads(k, bd)) * wmat
        q_c = _dot(q, c_st)
        q_n = _dot(q * n_st, bones)
        yield
        sb = s.astype(BF16)
        num = sc * q_c + _dot(sb, _tile_heads(v, bd))
        den = sc * q_n + _dot(sb, bones)
        yield
        hh = num / jnp.maximum(jnp.abs(den), jnp.exp(-m))

        b_last = b[CHUNK - 1:CHUNK]
        g = b_last - b + li
        m_new = jnp.maximum(b_last + m_st, jnp.max(g, axis=0, keepdims=True))
        ws = jnp.exp(g - m_new)
        dec = jnp.exp(b_last + m_st - m_new)
        kw = k * ws
        kv = _dot_tn(kw, v)
        ms = _head_mean(hh * hh, bones)
        yield
        c_sc[sq] = dec * c_st + jnp.where(bd, kv, 0.0)
        n_sc[sq] = dec * n_st + jnp.sum(kw, axis=0, keepdims=True)
        m_sc[sq] = m_new
        y_ref[sq, pl.ds(r0, CHUNK), :] = _sigmoid(o) * (hh * lax.rsqrt(ms + RMS_EPS) * ng)

    def chunk(c, carry):
        r0 = pl.multiple_of(c * CHUNK, CHUNK)
        _round_robin(chunk_seq(c, r0, sq) for sq in range(nseq))
        return carry

    lax.fori_loop(0, tb // CHUNK, chunk, 0)
    halo_sc[...] = pm_ref[:, tb - 8:tb, 0:2 * W_MIX]


def _mlstm(pm, conv_w, conv_b, gate_b, norm_g, tb, nseq):
    bsz, tp, _ = pm.shape
    return pl.pallas_call(
        functools.partial(_mlstm_kernel, tb=tb, nseq=nseq),
        grid=(bsz // nseq, tp // tb),
        in_specs=[pl.BlockSpec((nseq, tb, PM_COLS), lambda b, i: (b, i, 0)),
                  _const_spec((MLSTM_CONV, 2 * W_MIX)), _const_spec((1, 2 * W_MIX)),
                  _const_spec((2, W_MIX)), _const_spec((1, W_MIX))],
        out_specs=pl.BlockSpec((nseq, tb, W_MIX), lambda b, i: (b, i, 0)),
        out_shape=jax.ShapeDtypeStruct((bsz, tp, W_MIX), F32),
        scratch_shapes=[pltpu.VMEM((nseq, W_MIX, W_MIX), F32), pltpu.VMEM((nseq, 1, W_MIX), F32),
                        pltpu.VMEM((nseq, 1, W_MIX), F32), pltpu.VMEM((nseq, 8, 2 * W_MIX), F32)],
        compiler_params=_params(("parallel", "arbitrary")),
        name="mlstm",
    )(pm, conv_w, conv_b, gate_b, norm_g)


def _hgrn2_kernel(ph_ref, lb_ref, ng_ref, y_ref, s_sc, *, tb, nseq):
    blk = pl.program_id(1)

    @pl.when(blk == 0)
    def _():
        s_sc[...] = jnp.zeros_like(s_sc)

    bd, bones, _, _, tri = _head_consts()
    t16 = _iota((SUB, W_MIX), 0)
    lb = lb_ref[...]
    ng = ng_ref[...]
    log_lb = jnp.log(jnp.maximum(lb, LB_FLOOR))
    log1m_lb = jnp.log1p(-lb)

    def chunk_seq(r0, sq):
        cur = ph_ref[sq, pl.ds(r0, CHUNK), :]
        valid = (blk * tb + r0 + _iota((CHUNK, 1), 0)) >= PAD_FRONT
        z = cur[:, W_MIX:2 * W_MIX]
        q = jnp.where(valid, _silu(cur[:, 0:W_MIX]), 0.0)
        k = jnp.where(valid, (1.0 - lb) * _sigmoid(-z), 0.0)
        v = cur[:, 2 * W_MIX:3 * W_MIX]
        gate = cur[:, 3 * W_MIX:4 * W_MIX]
        bb = log1m_lb + _log_sigmoid(z)
        lf = jnp.maximum(log_lb, bb) + jnp.log(1.0 + jnp.exp(-jnp.abs(log_lb - bb)))
        lf = jnp.where(valid, lf, 0.0)
        g = _dot01(tri, lf)
        yield

        st = s_sc[sq]
        outs = []
        for i in range(CHUNK // SUB):
            lo = i * SUB
            gl = g[lo:lo + SUB] if i == 0 else g[lo:lo + SUB] - g[lo - 1:lo]
            qi, ki, vi = q[lo:lo + SUB], k[lo:lo + SUB], v[lo:lo + SUB]
            g_end = gl[SUB - 1:SUB]
            o = _dot_nt(qi * jnp.exp(gl), st)
            rows = []
            for s in range(SUB):
                d = jnp.where(t16 >= s, gl - gl[s:s + 1], NEG)
                rows.append(qi * ki[s:s + 1] * jnp.exp(d))
            att = _dot(jnp.concatenate(rows, axis=0), bones)
            kt = ki * jnp.exp(g_end - gl)
            vk = _dot_tn(vi, kt)
            yield
            for s in range(SUB):
                o = o + att[s * SUB:(s + 1) * SUB] * vi[s:s + 1]
            outs.append(o)
            st = jnp.exp(g_end) * st + jnp.where(bd, vk, 0.0)
        s_sc[sq] = st
        o = jnp.concatenate(outs, axis=0)
        ms = _head_mean(o * o, bones)
        yield
        y_ref[sq, pl.ds(r0, CHUNK), :] = o * lax.rsqrt(ms + RMS_EPS) * ng * _silu(gate)

    def chunk(c, carry):
        r0 = pl.multiple_of(c * CHUNK, CHUNK)
        _round_robin(chunk_seq(r0, sq) for sq in range(nseq))
        return carry

    lax.fori_loop(0, tb // CHUNK, chunk, 0)


def _hgrn2(ph, lb, norm_g, tb, nseq):
    bsz, tp, _ = ph.shape
    return pl.pallas_call(
        functools.partial(_hgrn2_kernel, tb=tb, nseq=nseq),
        grid=(bsz // nseq, tp // tb),
        in_specs=[pl.BlockSpec((nseq, tb, H_COLS), lambda b, i: (b, i, 0)),
                  _const_spec((1, W_MIX)), _const_spec((1, W_MIX))],
        out_specs=pl.BlockSpec((nseq, tb, W_MIX), lambda b, i: (b, i, 0)),
        out_shape=jax.ShapeDtypeStruct((bsz, tp, W_MIX), F32),
        scratch_shapes=[pltpu.VMEM((nseq, W_MIX, W_MIX), F32)],
        compiler_params=_params(("parallel", "arbitrary")),
        name="hgrn2",
    )(ph, lb, norm_g)


def _rwkv7_kernel(*refs, tb, first, nseq):
    if first:
        (pr_ref, mu_ref, wc_ref, vec_ref, y_ref, vf_out_ref, s_sc, halo_sc) = refs
    else:
        (pr_ref, vf_ref, mu_ref, wc_ref, vec_ref, v1_ref, v2_ref, y_ref, s_sc, halo_sc) = refs
    blk = pl.program_id(1)

    @pl.when(blk == 0)
    def _():
        s_sc[...] = jnp.zeros_like(s_sc)
        halo_sc[...] = jnp.zeros_like(halo_sc)

    bd, bones, t_io, s_io, tri = _head_consts()
    causal = t_io >= s_io
    strict = t_io > s_io
    eye4 = (t_io == s_io).astype(F32)
    lane = _iota((CHUNK, W_MIX), 1)
    mu = mu_ref[...]
    vec = vec_ref[...]
    w0, a0, k_k, k_a, r_k, gn_w, gn_b, v0 = [vec[i:i + 1] for i in range(8)]

    def hmm(x, y):
        return _dot(x, _tile_heads(y, bd))

    def chunk_seq(c, r0, sq):
        cur = pr_ref[sq, pl.ds(r0, CHUNK), :]
        prev = pr_ref[sq, pl.ds(pl.multiple_of(jnp.maximum(r0 - 8, 0), 8), 8), :]
        prev = jnp.where(c == 0, halo_sc[sq], prev)
        shifted = jnp.concatenate([prev, cur], axis=0)[7:7 + CHUNK]
        xm = cur + (shifted - cur) * mu
        r = xm[:, 0:W_MIX]
        k = xm[:, W_MIX:2 * W_MIX]
        v = xm[:, 2 * W_MIX:3 * W_MIX]
        cc = xm[:, 3 * W_MIX:4 * W_MIX]
        feat = jnp.where(lane < RANK_W, jnp.tanh(cc),
                         jnp.where(lane < RANK_W + RANK_A, cc, _sigmoid(cc)))
        proj = _dot(feat, wc_ref[...])
        yield
        w_raw = -_softplus(-(w0 + proj[:, 0:W_MIX])) - 0.5
        a_gate = _sigmoid(a0 + proj[:, W_MIX:2 * W_MIX])
        g_out = proj[:, 2 * W_MIX:3 * W_MIX]
        if first:
            vf_out_ref[sq, pl.ds(r0, CHUNK), :] = v
        else:
            vf = vf_ref[sq, pl.ds(r0, CHUNK), :]
            vlow = _dot(v, v1_ref[...])
            yield
            mix = _sigmoid(v0 + _dot(vlow, v2_ref[...]))
            v = v + (vf - v) * mix
        kk = k * k_k
        kk_sq = _dot(kk * kk, bones)
        lw = -jnp.exp(w_raw)
        cum = _dot01(tri, lw)
        yield
        kk = kk / jnp.maximum(jnp.sqrt(kk_sq), L2_EPS)
        k = k * (1.0 + (a_gate - 1.0) * k_a)
        a = -kk
        b = kk * a_gate

        tot = cum[CHUNK - 1:CHUNK]
        e_neg = jnp.exp(-cum)
        e_end = jnp.exp(tot - cum)
        a_t = a * jnp.exp(cum - lw)
        r_t = r * jnp.exp(cum)
        ar = jnp.concatenate([a_t, r_t], axis=0)
        nb = _dot_nt(ar, _tile_heads(b * e_neg, bd))
        nk = _dot_nt(ar, _tile_heads(k * e_neg, bd))
        yield
        n_ab = jnp.where(strict, nb[0:CHUNK], 0.0)
        m_rb = jnp.where(causal, nb[CHUNK:], 0.0)
        n_ak = jnp.where(strict, nk[0:CHUNK], 0.0)
        m_rk = jnp.where(causal, nk[CHUNK:], 0.0)

        p = n_ab
        tinv = eye4 + p
        s_st = s_sc[sq]
        nv = hmm(n_ak, v)
        y = _dot_nt(r_t, s_st) + hmm(m_rk, v)
        vk = _dot_tn(v, k * e_end)
        bonus = _dot(r * k * r_k, bones)
        for i in range(5):
            yield
            p_sq = hmm(p, p)
            if i > 0:
                tinv = hmm(tinv, eye4 + p)
            p = p_sq
        yield
        tinv = hmm(tinv, eye4 + p)
        yield
        a_hat = hmm(tinv, a_t)
        v_hat = hmm(tinv, nv)
        yield
        u = _dot_nt(a_hat, s_st) + v_hat
        yield
        y = y + hmm(m_rb, u)
        ub = _dot_tn(u, b * e_end)
        yield
        s_sc[sq] = s_st * jnp.exp(tot) + jnp.where(bd, ub + vk, 0.0)
        mean = _head_mean(y, bones)
        yield
        yc = y - mean
        var = _head_mean(yc * yc, bones)
        yield
        yn = yc * lax.rsqrt(var + GN_EPS) * gn_w + gn_b
        y_ref[sq, pl.ds(r0, CHUNK), :] = (yn + bonus * v) * g_out

    def chunk(c, carry):
        r0 = pl.multiple_of(c * CHUNK, CHUNK)
        _round_robin(chunk_seq(c, r0, sq) for sq in range(nseq))
        return carry

    lax.fori_loop(0, tb // CHUNK, chunk, 0)
    halo_sc[...] = pr_ref[:, tb - 8:tb, :]


def _rwkv7(pr, v_first, mu, wc, vec, v1, v2, tb, nseq):
    bsz, tp, _ = pr.shape
    first = v_first is None
    blk = lambda n: pl.BlockSpec((nseq, tb, n), lambda b, i: (b, i, 0))
    yshape = jax.ShapeDtypeStruct((bsz, tp, W_MIX), F32)
    common = [_const_spec((1, R_COLS)), _const_spec((W_MIX, 3 * W_MIX)), _const_spec((8, W_MIX))]
    if first:
        in_specs = [blk(R_COLS)] + common
        args = (pr, mu, wc, vec)
        out_specs, out_shape = [blk(W_MIX), blk(W_MIX)], [yshape, yshape]
    else:
        in_specs = [blk(R_COLS), blk(W_MIX)] + common + [_const_spec((W_MIX, RANK_V)),
                                                          _const_spec((RANK_V, W_MIX))]
        args = (pr, v_first, mu, wc, vec, v1, v2)
        out_specs, out_shape = blk(W_MIX), yshape
    out = pl.pallas_call(
        functools.partial(_rwkv7_kernel, tb=tb, first=first, nseq=nseq),
        grid=(bsz // nseq, tp // tb),
        in_specs=in_specs, out_specs=out_specs, out_shape=out_shape,
        scratch_shapes=[pltpu.VMEM((nseq, W_MIX, W_MIX), F32), pltpu.VMEM((nseq, 8, R_COLS), F32)],
        compiler_params=_params(("parallel", "arbitrary")),
        name="rwkv7_first" if first else "rwkv7",
    )(*args)
    return (out[0], out[1]) if first else (out, v_first)


def _s5_disc_kernel(are_ref, aim_ref, ls_ref, bre_ref, bim_ref, ar_ref, ai_ref, bbre_ref, bbim_ref):
    a_re = are_ref[...]
    a_im = aim_ref[...]
    step = jnp.exp(ls_ref[...])
    mag = jnp.exp(a_re * step)
    ang = a_im * step
    ab_re = mag * jnp.cos(ang)
    ab_im = mag * jnp.sin(ang)
    den = a_re * a_re + a_im * a_im
    num_re = ab_re - 1.0
    coef_re = (num_re * a_re + ab_im * a_im) / den
    coef_im = (ab_im * a_re - num_re * a_im) / den
    ar_ref[...] = ab_re
    ai_ref[...] = ab_im
    b_re = bre_ref[...]
    b_im = bim_ref[...]
    bbre_ref[...] = coef_re[None] * b_re - coef_im[None] * b_im
    bbim_ref[...] = coef_re[None] * b_im + coef_im[None] * b_re


def _s5_discretize(a_re, a_im, log_step, b_re_t, b_im_t):
    nl = a_re.shape[0]
    gp = pl.BlockSpec((None, S5_GROUPS, S5_STATE), lambda l: (l, 0, 0))
    cgp = pl.BlockSpec((None, S5_GROUP, S5_GROUPS, S5_STATE), lambda l: (l, 0, 0, 0))
    return pl.pallas_call(
        _s5_disc_kernel,
        grid=(nl,),
        in_specs=[gp, gp, pl.BlockSpec((None, S5_GROUPS, 1), lambda l: (l, 0, 0)), cgp, cgp],
        out_specs=[gp, gp, cgp, cgp],
        out_shape=[jax.ShapeDtypeStruct(a_re.shape, F32), jax.ShapeDtypeStruct(a_re.shape, F32),
                   jax.ShapeDtypeStruct(b_re_t.shape, F32), jax.ShapeDtypeStruct(b_re_t.shape, F32)],
        compiler_params=_params(("parallel",)),
        name="s5_discretize",
    )(a_re, a_im, log_step, b_re_t, b_im_t)


def _s5_kernel(u_ref, wbu_ref, ar_ref, ai_ref, wcre_ref, wcim_ref, d_ref, wglu_ref, bglu_ref,
               y_ref, x_sc, st_sc, *, tb, nb):
    @pl.when(pl.program_id(0) == 0)
    def _():
        st_sc[...] = jnp.zeros_like(st_sc)

    u = u_ref[...]
    x_sc[...] = _dot(u, wbu_ref[...])
    ar = jnp.broadcast_to(ar_ref[...], (nb, S5_N))
    ai = jnp.broadcast_to(ai_ref[...], (nb, S5_N))

    def step(t, carry):
        xr, xi = carry
        rows = pl.ds(pl.multiple_of(t * nb, nb), nb)
        nr = ar * xr - ai * xi + x_sc[rows, 0:S5_N]
        ni = ar * xi + ai * xr + x_sc[rows, S5_N:2 * S5_N]
        x_sc[rows, 0:S5_N] = nr
        x_sc[rows, S5_N:2 * S5_N] = ni
        return nr, ni

    xr, xi = lax.fori_loop(0, tb, step, (st_sc[:, 0:S5_N], st_sc[:, S5_N:2 * S5_N]), unroll=8)
    st_sc[:, 0:S5_N] = xr
    st_sc[:, S5_N:2 * S5_N] = xi
    y = _dot(x_sc[:, 0:S5_N], wcre_ref[...]) - _dot(x_sc[:, S5_N:2 * S5_N], wcim_ref[...])
    y = _gelu_tanh(y + d_ref[...] * u)
    y_ref[...] = y * _sigmoid(_dot(y, wglu_ref[...]) + bglu_ref[...])


def _s5(u_tb, wbu, ar, ai, wcre, wcim, d_skip, wglu, bglu, nb, tb):
    rows = u_tb.shape[0]
    tp = rows // nb
    return pl.pallas_call(
        functools.partial(_s5_kernel, tb=tb, nb=nb),
        grid=(tp // tb,),
        in_specs=[pl.BlockSpec((tb * nb, W_MIX), lambda i: (i, 0)),
                  _const_spec((W_MIX, 2 * S5_N)), _const_spec((1, S5_N)), _const_spec((1, S5_N)),
                  _const_spec((S5_N, W_MIX)), _const_spec((S5_N, W_MIX)), _const_spec((1, W_MIX)),
                  _const_spec((W_MIX, W_MIX)), _const_spec((1, W_MIX))],
        out_specs=pl.BlockSpec((tb * nb, W_MIX), lambda i: (i, 0)),
        out_shape=jax.ShapeDtypeStruct((rows, W_MIX), F32),
        scratch_shapes=[pltpu.VMEM((tb * nb, 2 * S5_N), F32), pltpu.VMEM((nb, 2 * S5_N), F32)],
        compiler_params=_params(("arbitrary",)),
        name="s5",
    )(u_tb, wbu, ar, ai, wcre, wcim, d_skip, wglu, bglu)


def _merge_kernel(x_ref, ym_ref, yr_ref, yh_ref, ys_ref, g0_ref, g1_ref, wg_ref, bg_ref, wbr_ref,
                  wo_ref, o_ref):
    x = x_ref[...]
    ub = _rmsnorm_rows(x, g0_ref[...]).astype(BF16)
    merged = None
    for n, y_ref in enumerate((ym_ref, yr_ref, yh_ref, ys_ref)):
        gate = _sigmoid(jnp.dot(ub, wg_ref[n], preferred_element_type=F32) + bg_ref[n])
        term = gate * _dot(y_ref[...], wbr_ref[n])
        merged = term if merged is None else merged + term
    o_ref[...] = x + _rmsnorm_rows(_dot(merged, wo_ref[...]), g1_ref[...])


def _merge(h, ym, yr, yh, ys_tm, g0, g1, wg, bg, wbr, wo, tm):
    bsz, tp, d = h.shape
    row = lambda n: pl.BlockSpec((None, tm, n), lambda b, i: (b, i, 0))
    return pl.pallas_call(
        _merge_kernel,
        grid=(bsz, tp // tm),
        in_specs=[row(d), row(W_MIX), row(W_MIX), row(W_MIX),
                  pl.BlockSpec((tm, W_MIX), lambda b, i: (i, b)),
                  _const_spec((1, d)), _const_spec((1, d)),
                  _const_spec((N_BRANCH, d, d)), _const_spec((N_BRANCH, 1, d)),
                  _const_spec((N_BRANCH, W_MIX, d)), _const_spec((d, d))],
        out_specs=row(d),
        out_shape=jax.ShapeDtypeStruct(h.shape, F32),
        compiler_params=_params(("parallel", "parallel")),
        name="merge",
    )(h, ym, yr, yh, ys_tm, g0, g1, wg, bg, wbr, wo)


FFN_COLS = 256


def _ffn_kernel(x_ref, g2_ref, g3_ref, wup_ref, cw_ref, cb_ref, wdn_ref, o_ref, halo_sc, act_sc, *, tm):
    i = pl.program_id(1)

    @pl.when(i == 0)
    def _():
        halo_sc[...] = jnp.zeros_like(halo_sc)

    x = x_ref[...]
    u = _rmsnorm_rows(x, g2_ref[...])
    valid = (i * tm + _iota((tm, 1), 0)) >= PAD_FRONT
    ub = jnp.where(valid, u, 0.0).astype(BF16)

    def conv(c0):
        z = jnp.dot(ub, wup_ref[:, c0:c0 + FFN_COLS], preferred_element_type=F32)
        ext = jnp.concatenate([halo_sc[:, c0:c0 + FFN_COLS], z], axis=0)
        halo_sc[:, c0:c0 + FFN_COLS] = z[tm - 8:tm]
        w = cw_ref[:, c0:c0 + FFN_COLS]
        return (cb_ref[:, c0:c0 + FFN_COLS] + w[2:3] * z + w[1:2] * ext[7:7 + tm]
                + w[0:1] * ext[6:6 + tm])

    for j in range(D_FF // FFN_COLS):
        a = _gelu_tanh(conv(j * FFN_COLS)) * conv(D_FF + j * FFN_COLS)
        act_sc[:, j * FFN_COLS:(j + 1) * FFN_COLS] = a.astype(BF16)
    down = jnp.dot(act_sc[...], wdn_ref[...], preferred_element_type=F32)
    o_ref[...] = x + _rmsnorm_rows(down, g3_ref[...])


def _ffn(h, g2, g3, wup, conv_w, conv_b, wdn, tm):
    bsz, tp, d = h.shape
    row = pl.BlockSpec((None, tm, d), lambda b, i: (b, i, 0))
    return pl.pallas_call(
        functools.partial(_ffn_kernel, tm=tm),
        grid=(bsz, tp // tm),
        in_specs=[row, _const_spec((1, d)), _const_spec((1, d)), _const_spec((d, 2 * D_FF)),
                  _const_spec((FFN_CONV, 2 * D_FF)), _const_spec((1, 2 * D_FF)),
                  _const_spec((D_FF, d))],
        out_specs=row,
        out_shape=jax.ShapeDtypeStruct(h.shape, F32),
        scratch_shapes=[pltpu.VMEM((8, 2 * D_FF), F32), pltpu.VMEM((tm, D_FF), BF16)],
        compiler_params=_params(("parallel", "arbitrary")),
        name="conv_ffn",
    )(h, g2, g3, wup, conv_w, conv_b, wdn)


def _block_diag_in(bb):
    eye = jnp.eye(S5_GROUPS, dtype=F32)
    return jnp.einsum('cgp,gh->gchp', bb, eye).reshape(W_MIX, S5_N)


def _block_diag_out(c):
    eye = jnp.eye(S5_GROUPS, dtype=F32)
    return jnp.einsum('gcp,gh->gphc', c, eye).reshape(S5_N, W_MIX)


def kernel(x, meta, norms, w_in, w_gate, b_gate, w_branch, w_out, m_conv_w, m_conv_b, m_gate_b, m_norm,
           r_mu, r_w0, r_w2, r_a0, r_a2, r_g2, r_kk, r_ka, r_rk, r_gn_w, r_gn_b, r_v0, r_v1, r_v2,
           h_lb, h_norm, s_a_re, s_a_im, s_b_re, s_b_im, s_c_re, s_c_im, s_d, s_log_step, s_w_glu, s_b_glu,
           f_up, f_conv_w, f_conv_b, f_down):
    bsz, seq, d = x.shape
    depth = w_in.shape[0]
    t_all = N_META + seq
    tp = PAD_FRONT + t_all
    assert d == D_MODEL and tp % CHUNK == 0
    tm = _row_tile(tp, 688)
    tb = _chunk_block(tp, 192)
    ts5 = _chunk_block(tp, 192)
    nseq = NSEQ if bsz % NSEQ == 0 else 1

    h = jnp.concatenate([jnp.zeros((bsz, PAD_FRONT, d), F32),
                         jnp.broadcast_to(meta.astype(F32)[None], (bsz, N_META, d)), x], axis=1)

    lb_w = jax.nn.softmax(h_lb.astype(F32), axis=0)
    lbs = jnp.cumsum(lb_w, axis=0) - lb_w[0:1]

    ar, ai, bbre, bbim = _s5_discretize(
        s_a_re, s_a_im, s_log_step[..., None],
        jnp.transpose(s_b_re, (0, 3, 1, 2)), jnp.transpose(s_b_im, (0, 3, 1, 2)))

    row = lambda a: a.reshape(1, -1)
    v_first = None
    for l in range(depth):
        w = w_in[l]
        w_r = jnp.concatenate(
            [w[:, 0:4 * W_MIX],
             jnp.repeat(w[:, 4 * W_MIX:4 * W_MIX + N_HEADS], HEAD_DIM, axis=1),
             jnp.repeat(w[:, 4 * W_MIX + N_HEADS:M_COLS], HEAD_DIM, axis=1),
             w[:, M_COLS:]], axis=1).astype(BF16)
        pm, pr, ph, ps_tm = _proj_in(h, row(norms[l, 0]), w_r, tm)

        ym = _mlstm(pm, m_conv_w[l], row(m_conv_b[l]), jnp.repeat(m_gate_b[l], HEAD_DIM, axis=1),
                    row(m_norm[l]), tb, nseq)

        wc = jnp.zeros((W_MIX, 3 * W_MIX), F32)
        wc = wc.at[0:RANK_W, 0:W_MIX].set(r_w2[l])
        wc = wc.at[RANK_W:RANK_W + RANK_A, W_MIX:2 * W_MIX].set(r_a2[l])
        wc = wc.at[RANK_W + RANK_A:, 2 * W_MIX:].set(r_g2[l]).astype(BF16)
        v0 = r_v0[l - 1] if l > 0 else jnp.zeros((W_MIX,), F32)
        vec = jnp.stack([r_w0[l], r_a0[l], r_kk[l], r_ka[l], r_rk[l], r_gn_w[l], r_gn_b[l], v0])
        if l == 0:
            yr, v_first = _rwkv7(pr, None, row(r_mu[l]), wc, vec, None, None, tb, nseq)
        else:
            yr, _ = _rwkv7(pr, v_first, row(r_mu[l]), wc, vec, r_v1[l - 1].astype(BF16),
                           r_v2[l - 1].astype(BF16), tb, nseq)

        yh = _hgrn2(ph, row(lbs[l]), row(h_norm[l]), tb, nseq)

        wbu = jnp.concatenate([_block_diag_in(bbre[l]), _block_diag_in(bbim[l])], axis=1).astype(BF16)
        ys_tb = _s5(ps_tm.reshape(tp * bsz, W_MIX), wbu, row(ar[l]), row(ai[l]),
                    _block_diag_out(s_c_re[l]).astype(BF16), _block_diag_out(s_c_im[l]).astype(BF16),
                    row(s_d[l]), s_w_glu[l].astype(BF16), row(s_b_glu[l]), bsz, ts5)

        h = _merge(h, ym, yr, yh, ys_tb.reshape(tp, bsz * W_MIX), row(norms[l, 0]), row(norms[l, 1]),
                   w_gate[l].astype(BF16), b_gate[l][:, None, :], w_branch[l].astype(BF16),
                   w_out[l].astype(BF16), tm)
        h = _ffn(h, row(norms[l, 2]), row(norms[l, 3]), f_up[l].astype(BF16), f_conv_w[l],
                 row(f_conv_b[l]), f_down[l].astype(BF16), tm)
    return h[:, PAD_FRONT + N_META:]
```

```python
import functools
import math

import jax
import jax.numpy as jnp
from jax import lax
from jax.experimental import pallas as pl
from jax.experimental.pallas import tpu as pltpu

F32 = jnp.float32
BF16 = jnp.bfloat16

D_MODEL = 1024
N_META = 16
CHUNK = 64
PAD_FRONT = CHUNK - N_META
N_BRANCH = 4
W_MIX = D_MODEL // 4
HEAD_DIM = 64
N_HEADS = W_MIX // HEAD_DIM
MLSTM_CONV = 4
RANK_W = 64
RANK_A = 64
RANK_V = 32
RANK_G = 128
S5_GROUP = 16
S5_GROUPS = W_MIX // S5_GROUP
S5_STATE = 64
S5_N = S5_GROUPS * S5_STATE
D_FF = 256 * ((8 * D_MODEL // 3 + 255) // 256)
FFN_CONV = 3
M_COLS = 4 * W_MIX + 2 * N_HEADS
R_COLS = 3 * W_MIX + RANK_W + RANK_A + RANK_G
H_COLS = 4 * W_MIX
S_COLS = W_MIX
PH_COLS = 5 * W_MIX
PM_COLS = 6 * W_MIX
RMS_EPS = 1e-6
GN_EPS = 64e-5
L2_EPS = 1e-12
LB_FLOOR = 1e-30
NEG = -1e30
SUB = 16
PAIR = 2 * HEAD_DIM
NSEQ = 8

VMEM_LIMIT_BYTES = 56 * 1024 * 1024


def _dot(a, b):
    return jnp.dot(a.astype(BF16), b.astype(BF16), preferred_element_type=F32)


def _dot_nt(a, b):
    return lax.dot_general(a.astype(BF16), b.astype(BF16), (((1,), (1,)), ((), ())),
                           preferred_element_type=F32)


def _dot_tn(a, b):
    return lax.dot_general(a.astype(BF16), b.astype(BF16), (((0,), (0,)), ((), ())),
                           preferred_element_type=F32)


def _dot01(sel, x):
    hi = x.astype(BF16)
    r1 = x - hi.astype(F32)
    mid = r1.astype(BF16)
    lo = (r1 - mid.astype(F32)).astype(BF16)
    out = jnp.dot(sel, hi, preferred_element_type=F32)
    out = out + jnp.dot(sel, mid, preferred_element_type=F32)
    return out + jnp.dot(sel, lo, preferred_element_type=F32)


def _sigmoid(x):
    return 1.0 / (1.0 + jnp.exp(-x))


def _softplus(x):
    return jnp.maximum(x, 0.0) + jnp.log(1.0 + jnp.exp(-jnp.abs(x)))


def _log_sigmoid(x):
    return -_softplus(-x)


def _silu(x):
    return x * _sigmoid(x)


def _gelu_tanh(x):
    c = math.sqrt(2.0 / math.pi)
    return (0.5 * x) * (1.0 + jnp.tanh(x * (c + (c * 0.044715) * (x * x))))


def _rmsnorm_rows(x, g):
    ms = jnp.mean(x * x, axis=-1, keepdims=True)
    return x * lax.rsqrt(ms + RMS_EPS) * g


def _iota(shape, dim):
    return lax.broadcasted_iota(jnp.int32, shape, dim)


def _head_consts():
    r = _iota((PAIR, PAIR), 0)
    c = _iota((PAIR, PAIR), 1)
    bd2 = (r >> 6) == (c >> 6)
    t = _iota((CHUNK, W_MIX), 0)
    s = _iota((CHUNK, W_MIX), 1) & (HEAD_DIM - 1)
    tri = (_iota((CHUNK, CHUNK), 0) >= _iota((CHUNK, CHUNK), 1)).astype(BF16)
    return bd2, bd2.astype(BF16), t, s, tri


def _lanes(p):
    return slice(p * PAIR, (p + 1) * PAIR)


def _tile2(xp, bd2):
    return jnp.where(bd2, jnp.concatenate([xp, xp], axis=0), jnp.zeros((), BF16))


def _hmm(x, y, bd2):
    xb, yb = x.astype(BF16), y.astype(BF16)
    return jnp.concatenate(
        [jnp.dot(xb[:, _lanes(p)], _tile2(yb[:, _lanes(p)], bd2), preferred_element_type=F32)
         for p in range(2)], axis=1)


def _hmm_nt(x, y, bd2):
    xb, yb = x.astype(BF16), y.astype(BF16)
    return jnp.concatenate(
        [lax.dot_general(xb[:, _lanes(p)], _tile2(yb[:, _lanes(p)], bd2), (((1,), (1,)), ((), ())),
                         preferred_element_type=F32) for p in range(2)], axis=1)


def _state_dot(x, st):
    return jnp.concatenate([_dot(x[:, _lanes(p)], st[p]) for p in range(2)], axis=1)


def _state_dot_nt(x, st):
    return jnp.concatenate([_dot_nt(x[:, _lanes(p)], st[p]) for p in range(2)], axis=1)


def _outer(a, b, bd2):
    return [jnp.where(bd2, _dot_tn(a[:, _lanes(p)], b[:, _lanes(p)]), 0.0) for p in range(2)]


def _head_sum(x, bones2):
    xb = x.astype(BF16)
    return jnp.concatenate(
        [jnp.dot(xb[:, _lanes(p)], bones2, preferred_element_type=F32) for p in range(2)], axis=1)


def _head_mean(x, bones2):
    return _head_sum(x, bones2) * (1.0 / HEAD_DIM)


def _row_tile(tp, target):
    best = None
    for cand in range(16, tp + 1, 16):
        if tp % cand == 0 and cand <= target:
            best = cand
    return best if best is not None else tp


def _chunk_block(tp, target):
    best = CHUNK
    for cand in range(CHUNK, tp + 1, CHUNK):
        if tp % cand == 0 and cand <= target:
            best = cand
    return best


def _const_spec(shape):
    nd = len(shape)
    return pl.BlockSpec(shape, lambda *_: (0,) * nd)


def _round_robin(gens):
    gens = list(gens)
    while gens:
        alive = []
        for g in gens:
            try:
                next(g)
                alive.append(g)
            except StopIteration:
                pass
        gens = alive


def _params(sem):
    return pltpu.CompilerParams(dimension_semantics=sem, vmem_limit_bytes=VMEM_LIMIT_BYTES)


def _proj_in_kernel(x_ref, g_ref, w_ref, cw_ref, cb_ref, gb_ref, mu_ref, lb_ref,
                    pm_ref, pr_ref, ph_ref, ps_ref, halo_sc, *, tm):
    i = pl.program_id(1)

    @pl.when(i == 0)
    def _():
        halo_sc[...] = jnp.zeros_like(halo_sc)

    x = x_ref[...]
    u = _rmsnorm_rows(x, g_ref[...])
    valid = (i * tm + _iota((tm, 1), 0)) >= PAD_FRONT
    ub = jnp.where(valid, u, 0.0).astype(BF16)

    raw = jnp.dot(ub, w_ref[:, 0:PM_COLS], preferred_element_type=F32)
    qk_raw = raw[:, 0:2 * W_MIX]
    ext = jnp.concatenate([halo_sc[:, 0:2 * W_MIX], qk_raw], axis=0)
    halo_sc[:, 0:2 * W_MIX] = qk_raw[tm - 8:tm]
    cw = cw_ref[...]
    acc = cb_ref[...] + cw[MLSTM_CONV - 1:MLSTM_CONV] * qk_raw
    for j in range(MLSTM_CONV - 1):
        acc = acc + cw[j:j + 1] * ext[5 + j:5 + j + tm]
    qk = jnp.where(valid, _silu(acc), 0.0)
    gb = gb_ref[...]
    pm_ref[:, 0:W_MIX] = qk[:, 0:W_MIX] * (HEAD_DIM ** -0.5)
    pm_ref[:, W_MIX:2 * W_MIX] = qk[:, W_MIX:2 * W_MIX]
    pm_ref[:, 2 * W_MIX:3 * W_MIX] = raw[:, 2 * W_MIX:3 * W_MIX]
    pm_ref[:, 3 * W_MIX:4 * W_MIX] = _sigmoid(raw[:, 3 * W_MIX:4 * W_MIX])
    pm_ref[:, 4 * W_MIX:5 * W_MIX] = jnp.where(valid, raw[:, 4 * W_MIX:5 * W_MIX] + gb[0:1], NEG)
    pm_ref[:, 5 * W_MIX:6 * W_MIX] = jnp.where(
        valid, _log_sigmoid(raw[:, 5 * W_MIX:6 * W_MIX] + gb[1:2]), 0.0)

    c0 = PM_COLS
    raw = jnp.dot(ub, w_ref[:, c0:c0 + R_COLS], preferred_element_type=F32)
    ext = jnp.concatenate([halo_sc[:, 2 * W_MIX:2 * W_MIX + R_COLS], raw], axis=0)
    halo_sc[:, 2 * W_MIX:2 * W_MIX + R_COLS] = raw[tm - 8:tm]
    pr_ref[...] = raw + (ext[7:7 + tm] - raw) * mu_ref[...]

    c0 += R_COLS
    raw = jnp.dot(ub, w_ref[:, c0:c0 + H_COLS], preferred_element_type=F32)
    lb = lb_ref[...]
    z = raw[:, W_MIX:2 * W_MIX]
    bb = jnp.log1p(-lb) + _log_sigmoid(z)
    log_lb = jnp.log(jnp.maximum(lb, LB_FLOOR))
    lf = jnp.maximum(log_lb, bb) + jnp.log(1.0 + jnp.exp(-jnp.abs(log_lb - bb)))
    ph_ref[:, 0:W_MIX] = jnp.where(valid, _silu(raw[:, 0:W_MIX]), 0.0)
    ph_ref[:, W_MIX:2 * W_MIX] = jnp.where(valid, lf, 0.0)
    ph_ref[:, 2 * W_MIX:3 * W_MIX] = jnp.where(valid, (1.0 - lb) * _sigmoid(-z), 0.0)
    ph_ref[:, 3 * W_MIX:4 * W_MIX] = raw[:, 2 * W_MIX:3 * W_MIX]
    ph_ref[:, 4 * W_MIX:5 * W_MIX] = _silu(raw[:, 3 * W_MIX:4 * W_MIX])

    c0 += H_COLS
    ps_ref[...] = jnp.dot(ub, w_ref[:, c0:c0 + S_COLS], preferred_element_type=F32)


def _proj_in(h, g, w, conv_w, conv_b, gate_b, mu, lb, tm):
    bsz, tp, d = h.shape
    ncols = w.shape[1]
    row = lambda n: pl.BlockSpec((None, tm, n), lambda b, i: (b, i, 0))
    return pl.pallas_call(
        functools.partial(_proj_in_kernel, tm=tm),
        grid=(bsz, tp // tm),
        in_specs=[row(d), _const_spec((1, d)), _const_spec((d, ncols)),
                  _const_spec((MLSTM_CONV, 2 * W_MIX)), _const_spec((1, 2 * W_MIX)),
                  _const_spec((2, W_MIX)), _const_spec((1, R_COLS)), _const_spec((1, W_MIX))],
        out_specs=[row(PM_COLS), row(R_COLS), row(PH_COLS),
                   pl.BlockSpec((tm, S_COLS), lambda b, i: (i, b))],
        out_shape=[jax.ShapeDtypeStruct((bsz, tp, PM_COLS), F32),
                   jax.ShapeDtypeStruct((bsz, tp, R_COLS), F32),
                   jax.ShapeDtypeStruct((bsz, tp, PH_COLS), F32),
                   jax.ShapeDtypeStruct((tp, bsz * S_COLS), F32)],
        scratch_shapes=[pltpu.VMEM((8, 2 * W_MIX + R_COLS), F32)],
        compiler_params=_params(("parallel", "arbitrary")),
        name="proj_in",
    )(h, g, w, conv_w, conv_b, gate_b, mu, lb)


def _mlstm_kernel(pm_ref, ng_ref, y_ref, c_sc, n_sc, m_sc, *, tb, nseq):
    blk = pl.program_id(1)

    @pl.when(blk == 0)
    def _():
        c_sc[...] = jnp.zeros_like(c_sc)
        n_sc[...] = jnp.zeros_like(n_sc)
        m_sc[...] = jnp.zeros_like(m_sc)

    bd2, bones, t_io, s_io, tri = _head_consts()
    causal = t_io >= s_io
    eye4 = (t_io == s_io).astype(F32)
    ones64 = jnp.ones((CHUNK, CHUNK), BF16)
    head_of_lane = _iota((CHUNK, W_MIX), 1) >> 6
    ng = ng_ref[...]

    def chunk_seq(r0, sq):
        cur = pm_ref[sq, pl.ds(r0, CHUNK), :]
        q = cur[:, 0:W_MIX]
        k = cur[:, W_MIX:2 * W_MIX]
        v = cur[:, 2 * W_MIX:3 * W_MIX]
        o_gate = cur[:, 3 * W_MIX:4 * W_MIX]
        li = cur[:, 4 * W_MIX:5 * W_MIX]
        lf = cur[:, 5 * W_MIX:6 * W_MIX]

        c_st = [c_sc[sq, 0], c_sc[sq, 1]]
        n_st = n_sc[sq]
        m_st = m_sc[sq]

        b = _dot01(tri, lf)
        yield
        b_row = _dot01(ones64, b * eye4)
        li_row = _dot01(ones64, li * eye4)
        yield
        dmat = jnp.where(causal, b - b_row + li_row, NEG)
        mx = jnp.full((CHUNK, W_MIX), -3.0e38, F32)
        for h in range(N_HEADS):
            sel = head_of_lane == h
            mh = jnp.max(jnp.where(sel, dmat, -3.0e38), axis=-1, keepdims=True)
            mx = jnp.where(sel, mh, mx)
        inter = b + m_st
        m = jnp.maximum(inter, mx)
        wmat = jnp.exp(dmat - m)
        sc = jnp.exp(inter - m)
        s = _hmm_nt(q, k, bd2) * wmat
        q_c = _state_dot(q, c_st)
        q_n = _head_sum(q * n_st, bones)
        yield
        num = sc * q_c + _hmm(s, v, bd2)
        den = sc * q_n + _head_sum(s, bones)
        yield
        hh = num / jnp.maximum(jnp.abs(den), jnp.exp(-m))

        b_last = b[CHUNK - 1:CHUNK]
        g = b_last - b + li
        m_new = jnp.maximum(b_last + m_st, jnp.max(g, axis=0, keepdims=True))
        ws = jnp.exp(g - m_new)
        dec = jnp.exp(b_last + m_st - m_new)
        kw = k * ws
        kv = _outer(kw, v, bd2)
        ms = _head_mean(hh * hh, bones)
        yield
        for p in range(2):
            c_sc[sq, p] = dec[:, _lanes(p)] * c_st[p] + kv[p]
        n_sc[sq] = dec * n_st + jnp.sum(kw, axis=0, keepdims=True)
        m_sc[sq] = m_new
        y_ref[sq, pl.ds(r0, CHUNK), :] = o_gate * (hh * lax.rsqrt(ms + RMS_EPS) * ng)

    def chunk(c, carry):
        r0 = pl.multiple_of(c * CHUNK, CHUNK)
        _round_robin(chunk_seq(r0, sq) for sq in range(nseq))
        return carry

    lax.fori_loop(0, tb // CHUNK, chunk, 0)


def _mlstm(pm, norm_g, tb, nseq):
    bsz, tp, _ = pm.shape
    return pl.pallas_call(
        functools.partial(_mlstm_kernel, tb=tb, nseq=nseq),
        grid=(bsz // nseq, tp // tb),
        in_specs=[pl.BlockSpec((nseq, tb, PM_COLS), lambda b, i: (b, i, 0)), _const_spec((1, W_MIX))],
        out_specs=pl.BlockSpec((nseq, tb, W_MIX), lambda b, i: (b, i, 0)),
        out_shape=jax.ShapeDtypeStruct((bsz, tp, W_MIX), F32),
        scratch_shapes=[pltpu.VMEM((nseq, 2, PAIR, PAIR), F32), pltpu.VMEM((nseq, 1, W_MIX), F32),
                        pltpu.VMEM((nseq, 1, W_MIX), F32)],
        compiler_params=_params(("parallel", "arbitrary")),
        name="mlstm",
    )(pm, norm_g)


def _hgrn2_kernel(ph_ref, ng_ref, y_ref, s_sc, *, tb, nseq):
    blk = pl.program_id(1)

    @pl.when(blk == 0)
    def _():
        s_sc[...] = jnp.zeros_like(s_sc)

    bd2, bones, _, _, tri = _head_consts()
    t16 = _iota((SUB, W_MIX), 0)
    ng = ng_ref[...]

    def chunk_seq(r0, sq):
        cur = ph_ref[sq, pl.ds(r0, CHUNK), :]
        q = cur[:, 0:W_MIX]
        lf = cur[:, W_MIX:2 * W_MIX]
        k = cur[:, 2 * W_MIX:3 * W_MIX]
        v = cur[:, 3 * W_MIX:4 * W_MIX]
        gate = cur[:, 4 * W_MIX:5 * W_MIX]
        g = _dot01(tri, lf)
        yield

        st = [s_sc[sq, 0], s_sc[sq, 1]]
        outs = []
        for i in range(CHUNK // SUB):
            lo = i * SUB
            gl = g[lo:lo + SUB] if i == 0 else g[lo:lo + SUB] - g[lo - 1:lo]
            qi, ki, vi, fi = q[lo:lo + SUB], k[lo:lo + SUB], v[lo:lo + SUB], jnp.exp(lf[lo:lo + SUB])
            g_end = gl[SUB - 1:SUB]
            o = _state_dot_nt(qi * jnp.exp(gl), st)
            rows = [None] * SUB
            pw = jnp.where(t16 == SUB - 1, qi, 0.0)
            rows[SUB - 1] = pw * ki[SUB - 1:SUB]
            for s in range(SUB - 2, -1, -1):
                pw = jnp.where(t16 == s, qi, pw * fi[s + 1:s + 2])
                rows[s] = pw * ki[s:s + 1]
            att = _head_sum(jnp.concatenate(rows, axis=0), bones)
            kt = ki * jnp.exp(g_end - gl)
            vk = _outer(vi, kt, bd2)
            yield
            for s in range(SUB):
                o = o + att[s * SUB:(s + 1) * SUB] * vi[s:s + 1]
            outs.append(o)
            e_end = jnp.exp(g_end)
            st = [e_end[:, _lanes(p)] * st[p] + vk[p] for p in range(2)]
        for p in range(2):
            s_sc[sq, p] = st[p]
        o = jnp.concatenate(outs, axis=0)
        ms = _head_mean(o * o, bones)
        yield
        y_ref[sq, pl.ds(r0, CHUNK), :] = o * lax.rsqrt(ms + RMS_EPS) * ng * gate

    def chunk(c, carry):
        r0 = pl.multiple_of(c * CHUNK, CHUNK)
        _round_robin(chunk_seq(r0, sq) for sq in range(nseq))
        return carry

    lax.fori_loop(0, tb // CHUNK, chunk, 0)


def _hgrn2(ph, norm_g, tb, nseq):
    bsz, tp, _ = ph.shape
    return pl.pallas_call(
        functools.partial(_hgrn2_kernel, tb=tb, nseq=nseq),
        grid=(bsz // nseq, tp // tb),
        in_specs=[pl.BlockSpec((nseq, tb, PH_COLS), lambda b, i: (b, i, 0)), _const_spec((1, W_MIX))],
        out_specs=pl.BlockSpec((nseq, tb, W_MIX), lambda b, i: (b, i, 0)),
        out_shape=jax.ShapeDtypeStruct((bsz, tp, W_MIX), F32),
        scratch_shapes=[pltpu.VMEM((nseq, 2, PAIR, PAIR), F32)],
        compiler_params=_params(("parallel", "arbitrary")),
        name="hgrn2",
    )(ph, norm_g)


def _rwkv7_kernel(*refs, tb, first, nseq):
    if first:
        (pr_ref, wc_ref, vec_ref, y_ref, vf_out_ref, s_sc) = refs
    else:
        (pr_ref, vf_ref, wc_ref, vec_ref, v1_ref, v2_ref, y_ref, s_sc) = refs
    blk = pl.program_id(1)

    @pl.when(blk == 0)
    def _():
        s_sc[...] = jnp.zeros_like(s_sc)

    bd2, bones, t_io, s_io, tri = _head_consts()
    causal = t_io >= s_io
    strict = t_io > s_io
    eye4 = (t_io == s_io).astype(F32)
    lane = _iota((CHUNK, W_MIX), 1)
    vec = vec_ref[...]
    w0, a0, k_k, k_a, r_k, gn_w, gn_b, v0 = [vec[i:i + 1] for i in range(8)]

    def hmm(x, y):
        return _hmm(x, y, bd2)

    def chunk_seq(r0, sq):
        xm = pr_ref[sq, pl.ds(r0, CHUNK), :]
        r = xm[:, 0:W_MIX]
        k = xm[:, W_MIX:2 * W_MIX]
        v = xm[:, 2 * W_MIX:3 * W_MIX]
        cc = xm[:, 3 * W_MIX:4 * W_MIX]
        feat = jnp.where(lane < RANK_W, jnp.tanh(cc),
                         jnp.where(lane < RANK_W + RANK_A, cc, _sigmoid(cc)))
        proj = _dot(feat, wc_ref[...])
        yield
        w_raw = -_softplus(-(w0 + proj[:, 0:W_MIX])) - 0.5
        a_gate = _sigmoid(a0 + proj[:, W_MIX:2 * W_MIX])
        g_out = proj[:, 2 * W_MIX:3 * W_MIX]
        if first:
            vf_out_ref[sq, pl.ds(r0, CHUNK), :] = v
        else:
            vf = vf_ref[sq, pl.ds(r0, CHUNK), :]
            vlow = _dot(v, v1_ref[...])
            yield
            mix = _sigmoid(v0 + _dot(vlow, v2_ref[...]))
            v = v + (vf - v) * mix
        kk = k * k_k
        kk_sq = _head_sum(kk * kk, bones)
        lw = -jnp.exp(w_raw)
        cum = _dot01(tri, lw)
        yield
        kk = kk / jnp.maximum(jnp.sqrt(kk_sq), L2_EPS)
        k = k * (1.0 + (a_gate - 1.0) * k_a)
        a = -kk
        b = kk * a_gate

        tot = cum[CHUNK - 1:CHUNK]
        e_neg = jnp.exp(-cum)
        e_end = jnp.exp(tot - cum)
        a_t = a * jnp.exp(cum - lw)
        r_t = r * jnp.exp(cum)
        ar = jnp.concatenate([a_t, r_t], axis=0)
        nb = _hmm_nt(ar, b * e_neg, bd2)
        nk = _hmm_nt(ar, k * e_neg, bd2)
        yield
        n_ab = jnp.where(strict, nb[0:CHUNK], 0.0)
        m_rb = jnp.where(causal, nb[CHUNK:], 0.0)
        n_ak = jnp.where(strict, nk[0:CHUNK], 0.0)
        m_rk = jnp.where(causal, nk[CHUNK:], 0.0)

        pw = n_ab
        tinv = eye4 + pw
        s_st = [s_sc[sq, 0], s_sc[sq, 1]]
        nv = hmm(n_ak, v)
        y = _state_dot_nt(r_t, s_st) + hmm(m_rk, v)
        vk = _outer(v, k * e_end, bd2)
        bonus = _head_sum(r * k * r_k, bones)
        for i in range(5):
            yield
            pw_sq = hmm(pw, pw)
            if i > 0:
                tinv = hmm(tinv, eye4 + pw)
            pw = pw_sq
        yield
        tinv = hmm(tinv, eye4 + pw)
        yield
        a_hat = hmm(tinv, a_t)
        v_hat = hmm(tinv, nv)
        yield
        u = _state_dot_nt(a_hat, s_st) + v_hat
        yield
        y = y + hmm(m_rb, u)
        ub = _outer(u, b * e_end, bd2)
        yield
        e_tot = jnp.exp(tot)
        for p in range(2):
            s_sc[sq, p] = s_st[p] * e_tot[:, _lanes(p)] + ub[p] + vk[p]
        mean = _head_mean(y, bones)
        yield
        yc = y - mean
        var = _head_mean(yc * yc, bones)
        yield
        yn = yc * lax.rsqrt(var + GN_EPS) * gn_w + gn_b
        y_ref[sq, pl.ds(r0, CHUNK), :] = (yn + bonus * v) * g_out

    def chunk(c, carry):
        r0 = pl.multiple_of(c * CHUNK, CHUNK)
        _round_robin(chunk_seq(r0, sq) for sq in range(nseq))
        return carry

    lax.fori_loop(0, tb // CHUNK, chunk, 0)


def _rwkv7(pr, v_first, wc, vec, v1, v2, tb, nseq):
    bsz, tp, _ = pr.shape
    first = v_first is None
    blk = lambda n: pl.BlockSpec((nseq, tb, n), lambda b, i: (b, i, 0))
    yshape = jax.ShapeDtypeStruct((bsz, tp, W_MIX), F32)
    common = [_const_spec((W_MIX, 3 * W_MIX)), _const_spec((8, W_MIX))]
    if first:
        in_specs = [blk(R_COLS)] + common
        args = (pr, wc, vec)
        out_specs, out_shape = [blk(W_MIX), blk(W_MIX)], [yshape, yshape]
    else:
        in_specs = [blk(R_COLS), blk(W_MIX)] + common + [_const_spec((W_MIX, RANK_V)),
                                                          _const_spec((RANK_V, W_MIX))]
        args = (pr, v_first, wc, vec, v1, v2)
        out_specs, out_shape = blk(W_MIX), yshape
    out = pl.pallas_call(
        functools.partial(_rwkv7_kernel, tb=tb, first=first, nseq=nseq),
        grid=(bsz // nseq, tp // tb),
        in_specs=in_specs, out_specs=out_specs, out_shape=out_shape,
        scratch_shapes=[pltpu.VMEM((nseq, 2, PAIR, PAIR), F32)],
        compiler_params=_params(("parallel", "arbitrary")),
        name="rwkv7_first" if first else "rwkv7",
    )(*args)
    return (out[0], out[1]) if first else (out, v_first)


def _s5_disc_kernel(are_ref, aim_ref, ls_ref, bre_ref, bim_ref, ar_ref, ai_ref, bbre_ref, bbim_ref):
    a_re = are_ref[...]
    a_im = aim_ref[...]
    step = jnp.exp(ls_ref[...])
    mag = jnp.exp(a_re * step)
    ang = a_im * step
    ab_re = mag * jnp.cos(ang)
    ab_im = mag * jnp.sin(ang)
    den = a_re * a_re + a_im * a_im
    num_re = ab_re - 1.0
    coef_re = (num_re * a_re + ab_im * a_im) / den
    coef_im = (ab_im * a_re - num_re * a_im) / den
    ar_ref[...] = ab_re
    ai_ref[...] = ab_im
    b_re = bre_ref[...]
    b_im = bim_ref[...]
    bbre_ref[...] = coef_re[None] * b_re - coef_im[None] * b_im
    bbim_ref[...] = coef_re[None] * b_im + coef_im[None] * b_re


def _s5_discretize(a_re, a_im, log_step, b_re_t, b_im_t):
    nl = a_re.shape[0]
    gp = pl.BlockSpec((None, S5_GROUPS, S5_STATE), lambda l: (l, 0, 0))
    cgp = pl.BlockSpec((None, S5_GROUP, S5_GROUPS, S5_STATE), lambda l: (l, 0, 0, 0))
    return pl.pallas_call(
        _s5_disc_kernel,
        grid=(nl,),
        in_specs=[gp, gp, pl.BlockSpec((None, S5_GROUPS, 1), lambda l: (l, 0, 0)), cgp, cgp],
        out_specs=[gp, gp, cgp, cgp],
        out_shape=[jax.ShapeDtypeStruct(a_re.shape, F32), jax.ShapeDtypeStruct(a_re.shape, F32),
                   jax.ShapeDtypeStruct(b_re_t.shape, F32), jax.ShapeDtypeStruct(b_re_t.shape, F32)],
        compiler_params=_params(("parallel",)),
        name="s5_discretize",
    )(a_re, a_im, log_step, b_re_t, b_im_t)


def _s5_kernel(u_ref, wbu_ref, ar_ref, ai_ref, wcre_ref, wcim_ref, d_ref, wglu_ref, bglu_ref,
               y_ref, x_sc, st_sc, *, tb, nb):
    @pl.when(pl.program_id(0) == 0)
    def _():
        st_sc[...] = jnp.zeros_like(st_sc)

    u = u_ref[...]
    x_sc[...] = _dot(u, wbu_ref[...])
    ar = jnp.broadcast_to(ar_ref[...], (nb, S5_N))
    ai = jnp.broadcast_to(ai_ref[...], (nb, S5_N))

    def step(t, carry):
        xr, xi = carry
        rows = pl.ds(pl.multiple_of(t * nb, nb), nb)
        nr = ar * xr - ai * xi + x_sc[rows, 0:S5_N]
        ni = ar * xi + ai * xr + x_sc[rows, S5_N:2 * S5_N]
        x_sc[rows, 0:S5_N] = nr
        x_sc[rows, S5_N:2 * S5_N] = ni
        return nr, ni

    xr, xi = lax.fori_loop(0, tb, step, (st_sc[:, 0:S5_N], st_sc[:, S5_N:2 * S5_N]), unroll=8)
    st_sc[:, 0:S5_N] = xr
    st_sc[:, S5_N:2 * S5_N] = xi
    y = _dot(x_sc[:, 0:S5_N], wcre_ref[...]) - _dot(x_sc[:, S5_N:2 * S5_N], wcim_ref[...])
    y = _gelu_tanh(y + d_ref[...] * u)
    y_ref[...] = y * _sigmoid(_dot(y, wglu_ref[...]) + bglu_ref[...])


def _s5(u_tb, wbu, ar, ai, wcre, wcim, d_skip, wglu, bglu, nb, tb):
    rows = u_tb.shape[0]
    tp = rows // nb
    return pl.pallas_call(
        functools.partial(_s5_kernel, tb=tb, nb=nb),
        grid=(tp // tb,),
        in_specs=[pl.BlockSpec((tb * nb, W_MIX), lambda i: (i, 0)),
                  _const_spec((W_MIX, 2 * S5_N)), _const_spec((1, S5_N)), _const_spec((1, S5_N)),
                  _const_spec((S5_N, W_MIX)), _const_spec((S5_N, W_MIX)), _const_spec((1, W_MIX)),
                  _const_spec((W_MIX, W_MIX)), _const_spec((1, W_MIX))],
        out_specs=pl.BlockSpec((tb * nb, W_MIX), lambda i: (i, 0)),
        out_shape=jax.ShapeDtypeStruct((rows, W_MIX), F32),
        scratch_shapes=[pltpu.VMEM((tb * nb, 2 * S5_N), F32), pltpu.VMEM((nb, 2 * S5_N), F32)],
        compiler_params=_params(("arbitrary",)),
        name="s5",
    )(u_tb, wbu, ar, ai, wcre, wcim, d_skip, wglu, bglu)


def _merge_kernel(x_ref, ym_ref, yr_ref, yh_ref, ys_ref, g0_ref, g1_ref, wg_ref, bg_ref, wbr_ref,
                  wo_ref, o_ref):
    x = x_ref[...]
    ub = _rmsnorm_rows(x, g0_ref[...]).astype(BF16)
    merged = None
    for n, y_ref in enumerate((ym_ref, yr_ref, yh_ref, ys_ref)):
        gate = _sigmoid(jnp.dot(ub, wg_ref[n], preferred_element_type=F32) + bg_ref[n])
        term = gate * _dot(y_ref[...], wbr_ref[n])
        merged = term if merged is None else merged + term
    o_ref[...] = x + _rmsnorm_rows(_dot(merged, wo_ref[...]), g1_ref[...])


def _merge(h, ym, yr, yh, ys_tm, g0, g1, wg, bg, wbr, wo, tm):
    bsz, tp, d = h.shape
    row = lambda n: pl.BlockSpec((None, tm, n), lambda b, i: (b, i, 0))
    return pl.pallas_call(
        _merge_kernel,
        grid=(bsz, tp // tm),
        in_specs=[row(d), row(W_MIX), row(W_MIX), row(W_MIX),
                  pl.BlockSpec((tm, W_MIX), lambda b, i: (i, b)),
                  _const_spec((1, d)), _const_spec((1, d)),
                  _const_spec((N_BRANCH, d, d)), _const_spec((N_BRANCH, 1, d)),
                  _const_spec((N_BRANCH, W_MIX, d)), _const_spec((d, d))],
        out_specs=row(d),
        out_shape=jax.ShapeDtypeStruct(h.shape, F32),
        compiler_params=_params(("parallel", "parallel")),
        name="merge",
    )(h, ym, yr, yh, ys_tm, g0, g1, wg, bg, wbr, wo)


FFN_COLS = 256


def _ffn_kernel(x_ref, g2_ref, g3_ref, wup_ref, cw_ref, cb_ref, wdn_ref, o_ref, halo_sc, act_sc, *, tm):
    i = pl.program_id(1)

    @pl.when(i == 0)
    def _():
        halo_sc[...] = jnp.zeros_like(halo_sc)

    x = x_ref[...]
    u = _rmsnorm_rows(x, g2_ref[...])
    valid = (i * tm + _iota((tm, 1), 0)) >= PAD_FRONT
    ub = jnp.where(valid, u, 0.0).astype(BF16)

    def conv(c0):
        z = jnp.dot(ub, wup_ref[:, c0:c0 + FFN_COLS], preferred_element_type=F32)
        ext = jnp.concatenate([halo_sc[:, c0:c0 + FFN_COLS], z], axis=0)
        halo_sc[:, c0:c0 + FFN_COLS] = z[tm - 8:tm]
        w = cw_ref[:, c0:c0 + FFN_COLS]
        return (cb_ref[:, c0:c0 + FFN_COLS] + w[2:3] * z + w[1:2] * ext[7:7 + tm]
                + w[0:1] * ext[6:6 + tm])

    for j in range(D_FF // FFN_COLS):
        a = _gelu_tanh(conv(j * FFN_COLS)) * conv(D_FF + j * FFN_COLS)
        act_sc[:, j * FFN_COLS:(j + 1) * FFN_COLS] = a.astype(BF16)
    down = jnp.dot(act_sc[...], wdn_ref[...], preferred_element_type=F32)
    o_ref[...] = x + _rmsnorm_rows(down, g3_ref[...])


def _ffn(h, g2, g3, wup, conv_w, conv_b, wdn, tm):
    bsz, tp, d = h.shape
    row = pl.BlockSpec((None, tm, d), lambda b, i: (b, i, 0))
    return pl.pallas_call(
        functools.partial(_ffn_kernel, tm=tm),
        grid=(bsz, tp // tm),
        in_specs=[row, _const_spec((1, d)), _const_spec((1, d)), _const_spec((d, 2 * D_FF)),
                  _const_spec((FFN_CONV, 2 * D_FF)), _const_spec((1, 2 * D_FF)),
                  _const_spec((D_FF, d))],
        out_specs=row,
        out_shape=jax.ShapeDtypeStruct(h.shape, F32),
        scratch_shapes=[pltpu.VMEM((8, 2 * D_FF), F32), pltpu.VMEM((tm, D_FF), BF16)],
        compiler_params=_params(("parallel", "arbitrary")),
        name="conv_ffn",
    )(h, g2, g3, wup, conv_w, conv_b, wdn)


def _block_diag_in(bb):
    eye = jnp.eye(S5_GROUPS, dtype=F32)
    return jnp.einsum('cgp,gh->gchp', bb, eye).reshape(W_MIX, S5_N)


def _block_diag_out(c):
    eye = jnp.eye(S5_GROUPS, dtype=F32)
    return jnp.einsum('gcp,gh->gphc', c, eye).reshape(S5_N, W_MIX)


def kernel(x, meta, norms, w_in, w_gate, b_gate, w_branch, w_out, m_conv_w, m_conv_b, m_gate_b, m_norm,
           r_mu, r_w0, r_w2, r_a0, r_a2, r_g2, r_kk, r_ka, r_rk, r_gn_w, r_gn_b, r_v0, r_v1, r_v2,
           h_lb, h_norm, s_a_re, s_a_im, s_b_re, s_b_im, s_c_re, s_c_im, s_d, s_log_step, s_w_glu, s_b_glu,
           f_up, f_conv_w, f_conv_b, f_down):
    bsz, seq, d = x.shape
    depth = w_in.shape[0]
    t_all = N_META + seq
    tp = PAD_FRONT + t_all
    assert d == D_MODEL and tp % CHUNK == 0
    tm = _row_tile(tp, 688)
    tb = _chunk_block(tp, 192)
    ts5 = _chunk_block(tp, 192)
    nseq = NSEQ if bsz % NSEQ == 0 else 1

    h = jnp.concatenate([jnp.zeros((bsz, PAD_FRONT, d), F32),
                         jnp.broadcast_to(meta.astype(F32)[None], (bsz, N_META, d)), x], axis=1)

    lb_w = jax.nn.softmax(h_lb.astype(F32), axis=0)
    lbs = jnp.cumsum(lb_w, axis=0) - lb_w[0:1]

    ar, ai, bbre, bbim = _s5_discretize(
        s_a_re, s_a_im, s_log_step[..., None],
        jnp.transpose(s_b_re, (0, 3, 1, 2)), jnp.transpose(s_b_im, (0, 3, 1, 2)))

    row = lambda a: a.reshape(1, -1)
    v_first = None
    for l in range(depth):
        w = w_in[l]
        w_r = jnp.concatenate(
            [w[:, 0:4 * W_MIX],
             jnp.repeat(w[:, 4 * W_MIX:4 * W_MIX + N_HEADS], HEAD_DIM, axis=1),
             jnp.repeat(w[:, 4 * W_MIX + N_HEADS:M_COLS], HEAD_DIM, axis=1),
             w[:, M_COLS:]], axis=1).astype(BF16)
        pm, pr, ph, ps_tm = _proj_in(h, row(norms[l, 0]), w_r, m_conv_w[l], row(m_conv_b[l]),
                                     jnp.repeat(m_gate_b[l], HEAD_DIM, axis=1), row(r_mu[l]),
                                     row(lbs[l]), tm)

        ym = _mlstm(pm, row(m_norm[l]), tb, nseq)

        wc = jnp.zeros((W_MIX, 3 * W_MIX), F32)
        wc = wc.at[0:RANK_W, 0:W_MIX].set(r_w2[l])
        wc = wc.at[RANK_W:RANK_W + RANK_A, W_MIX:2 * W_MIX].set(r_a2[l])
        wc = wc.at[RANK_W + RANK_A:, 2 * W_MIX:].set(r_g2[l]).astype(BF16)
        v0 = r_v0[l - 1] if l > 0 else jnp.zeros((W_MIX,), F32)
        vec = jnp.stack([r_w0[l], r_a0[l], r_kk[l], r_ka[l], r_rk[l], r_gn_w[l], r_gn_b[l], v0])
        if l == 0:
            yr, v_first = _rwkv7(pr, None, wc, vec, None, None, tb, nseq)
        else:
            yr, _ = _rwkv7(pr, v_first, wc, vec, r_v1[l - 1].astype(BF16),
                           r_v2[l - 1].astype(BF16), tb, nseq)

        yh = _hgrn2(ph, row(h_norm[l]), tb, nseq)

        wbu = jnp.concatenate([_block_diag_in(bbre[l]), _block_diag_in(bbim[l])], axis=1).astype(BF16)
        ys_tb = _s5(ps_tm.reshape(tp * bsz, W_MIX), wbu, row(ar[l]), row(ai[l]),
                    _block_diag_out(s_c_re[l]).astype(BF16), _block_diag_out(s_c_im[l]).astype(BF16),
                    row(s_d[l]), s_w_glu[l].astype(BF16), row(s_b_glu[l]), bsz, ts5)

        h = _merge(h, ym, yr, yh, ys_tb.reshape(tp, bsz * W_MIX), row(norms[l, 0]), row(norms[l, 1]),
                   w_gate[l].astype(BF16), b_gate[l][:, None, :], w_branch[l].astype(BF16),
                   w_out[l].astype(BF16), tm)
        h = _ffn(h, row(norms[l, 2]), row(norms[l, 3]), f_up[l].astype(BF16), f_conv_w[l],
                 row(f_conv_b[l]), f_down[l].astype(BF16), tm)
    return h[:, PAD_FRONT + N_META:]
```

```python
import functools
import math

import jax
import jax.numpy as jnp
from jax import lax
from jax.experimental import pallas as pl
from jax.experimental.pallas import tpu as pltpu

F32 = jnp.float32
BF16 = jnp.bfloat16

D_MODEL = 1024
N_META = 16
CHUNK = 64
PAD_FRONT = CHUNK - N_META
N_BRANCH = 4
W_MIX = D_MODEL // 4
HEAD_DIM = 64
N_HEADS = W_MIX // HEAD_DIM
MLSTM_CONV = 4
RANK_W = 64
RANK_A = 64
RANK_V = 32
RANK_G = 128
S5_GROUP = 16
S5_GROUPS = W_MIX // S5_GROUP
S5_STATE = 64
S5_N = S5_GROUPS * S5_STATE
D_FF = 256 * ((8 * D_MODEL // 3 + 255) // 256)
FFN_CONV = 3
M_COLS = 4 * W_MIX + 2 * N_HEADS
R_COLS = 3 * W_MIX + RANK_W + RANK_A + RANK_G
H_COLS = 4 * W_MIX
S_COLS = W_MIX
PH_COLS = 5 * W_MIX
PM_COLS = 6 * W_MIX
RMS_EPS = 1e-6
GN_EPS = 64e-5
L2_EPS = 1e-12
LB_FLOOR = 1e-30
NEG = -1e30
SUB = 16
PAIR = 2 * HEAD_DIM
NSEQ = 8

VMEM_LIMIT_BYTES = 56 * 1024 * 1024


def _dot(a, b):
    return jnp.dot(a.astype(BF16), b.astype(BF16), preferred_element_type=F32)


def _dot_nt(a, b):
    return lax.dot_general(a.astype(BF16), b.astype(BF16), (((1,), (1,)), ((), ())),
                           preferred_element_type=F32)


def _dot_tn(a, b):
    return lax.dot_general(a.astype(BF16), b.astype(BF16), (((0,), (0,)), ((), ())),
                           preferred_element_type=F32)


def _dot01(sel, x):
    hi = x.astype(BF16)
    r1 = x - hi.astype(F32)
    mid = r1.astype(BF16)
    lo = (r1 - mid.astype(F32)).astype(BF16)
    out = jnp.dot(sel, hi, preferred_element_type=F32)
    out = out + jnp.dot(sel, mid, preferred_element_type=F32)
    return out + jnp.dot(sel, lo, preferred_element_type=F32)


def _sigmoid(x):
    return 1.0 / (1.0 + jnp.exp(-x))


def _softplus(x):
    return jnp.maximum(x, 0.0) + jnp.log(1.0 + jnp.exp(-jnp.abs(x)))


def _log_sigmoid(x):
    return -_softplus(-x)


def _silu(x):
    return x * _sigmoid(x)


def _gelu_tanh(x):
    c = math.sqrt(2.0 / math.pi)
    return (0.5 * x) * (1.0 + jnp.tanh(x * (c + (c * 0.044715) * (x * x))))


def _rmsnorm_rows(x, g):
    ms = jnp.mean(x * x, axis=-1, keepdims=True)
    return x * lax.rsqrt(ms + RMS_EPS) * g


def _iota(shape, dim):
    return lax.broadcasted_iota(jnp.int32, shape, dim)


def _head_consts():
    r = _iota((PAIR, PAIR), 0)
    c = _iota((PAIR, PAIR), 1)
    bd2 = (r >> 6) == (c >> 6)
    t = _iota((CHUNK, W_MIX), 0)
    s = _iota((CHUNK, W_MIX), 1) & (HEAD_DIM - 1)
    tri = (_iota((CHUNK, CHUNK), 0) >= _iota((CHUNK, CHUNK), 1)).astype(BF16)
    return bd2, bd2.astype(BF16), t, s, tri


def _lanes(p):
    return slice(p * PAIR, (p + 1) * PAIR)


def _tile2(xp, bd2):
    return jnp.where(bd2, jnp.concatenate([xp, xp], axis=0), jnp.zeros((), BF16))


def _hmm(x, y, bd2):
    xb, yb = x.astype(BF16), y.astype(BF16)
    return jnp.concatenate(
        [jnp.dot(xb[:, _lanes(p)], _tile2(yb[:, _lanes(p)], bd2), preferred_element_type=F32)
         for p in range(2)], axis=1)


def _hmm_nt(x, y, bd2):
    xb, yb = x.astype(BF16), y.astype(BF16)
    return jnp.concatenate(
        [lax.dot_general(xb[:, _lanes(p)], _tile2(yb[:, _lanes(p)], bd2), (((1,), (1,)), ((), ())),
                         preferred_element_type=F32) for p in range(2)], axis=1)


def _state_dot(x, st):
    return jnp.concatenate([_dot(x[:, _lanes(p)], st[p]) for p in range(2)], axis=1)


def _state_dot_nt(x, st):
    return jnp.concatenate([_dot_nt(x[:, _lanes(p)], st[p]) for p in range(2)], axis=1)


def _outer(a, b, bd2):
    return [jnp.where(bd2, _dot_tn(a[:, _lanes(p)], b[:, _lanes(p)]), 0.0) for p in range(2)]


def _head_sum(x, bones2):
    xb = x.astype(BF16)
    return jnp.concatenate(
        [jnp.dot(xb[:, _lanes(p)], bones2, preferred_element_type=F32) for p in range(2)], axis=1)


def _row_tile(tp, target):
    best = None
    for cand in range(16, tp + 1, 16):
        if tp % cand == 0 and cand <= target:
            best = cand
    return best if best is not None else tp


def _chunk_block(tp, target):
    best = CHUNK
    for cand in range(CHUNK, tp + 1, CHUNK):
        if tp % cand == 0 and cand <= target:
            best = cand
    return best


def _const_spec(shape):
    nd = len(shape)
    return pl.BlockSpec(shape, lambda *_: (0,) * nd)


def _round_robin(gens, shared):
    gens = list(gens)
    sends = [None] * len(gens)
    while gens:
        reqs, alive = [], []
        for g, v in zip(gens, sends):
            try:
                reqs.append(g.send(v))
                alive.append(g)
            except StopIteration:
                pass
        gens = alive
        sends = [None] * len(gens)
        if reqs and reqs[0] is not None:
            xs = [r[1] for r in reqs]
            out = shared[reqs[0][0]](jnp.concatenate(xs, axis=0))
            n = xs[0].shape[0]
            sends = [out[i * n:(i + 1) * n] for i in range(len(xs))]


def _params(sem):
    return pltpu.CompilerParams(dimension_semantics=sem, vmem_limit_bytes=VMEM_LIMIT_BYTES)


def _proj_in_kernel(x_ref, g_ref, w_ref, cw_ref, cb_ref, gb_ref, mu_ref, lb_ref,
                    pm_ref, pr_ref, ph_ref, ps_ref, halo_sc, *, tm):
    i = pl.program_id(1)

    @pl.when(i == 0)
    def _():
        halo_sc[...] = jnp.zeros_like(halo_sc)

    x = x_ref[...]
    u = _rmsnorm_rows(x, g_ref[...])
    valid = (i * tm + _iota((tm, 1), 0)) >= PAD_FRONT
    ub = jnp.where(valid, u, 0.0).astype(BF16)

    raw = jnp.dot(ub, w_ref[:, 0:PM_COLS], preferred_element_type=F32)
    qk_raw = raw[:, 0:2 * W_MIX]
    ext = jnp.concatenate([halo_sc[:, 0:2 * W_MIX], qk_raw], axis=0)
    halo_sc[:, 0:2 * W_MIX] = qk_raw[tm - 8:tm]
    cw = cw_ref[...]
    acc = cb_ref[...] + cw[MLSTM_CONV - 1:MLSTM_CONV] * qk_raw
    for j in range(MLSTM_CONV - 1):
        acc = acc + cw[j:j + 1] * ext[5 + j:5 + j + tm]
    qk = jnp.where(valid, _silu(acc), 0.0)
    gb = gb_ref[...]
    pm_ref[:, 0:W_MIX] = qk[:, 0:W_MIX] * (HEAD_DIM ** -0.5)
    pm_ref[:, W_MIX:2 * W_MIX] = qk[:, W_MIX:2 * W_MIX]
    pm_ref[:, 2 * W_MIX:3 * W_MIX] = raw[:, 2 * W_MIX:3 * W_MIX]
    pm_ref[:, 3 * W_MIX:4 * W_MIX] = _sigmoid(raw[:, 3 * W_MIX:4 * W_MIX])
    pm_ref[:, 4 * W_MIX:5 * W_MIX] = jnp.where(valid, raw[:, 4 * W_MIX:5 * W_MIX] + gb[0:1], NEG)
    pm_ref[:, 5 * W_MIX:6 * W_MIX] = jnp.where(
        valid, _log_sigmoid(raw[:, 5 * W_MIX:6 * W_MIX] + gb[1:2]), 0.0)

    c0 = PM_COLS
    raw = jnp.dot(ub, w_ref[:, c0:c0 + R_COLS], preferred_element_type=F32)
    ext = jnp.concatenate([halo_sc[:, 2 * W_MIX:2 * W_MIX + R_COLS], raw], axis=0)
    halo_sc[:, 2 * W_MIX:2 * W_MIX + R_COLS] = raw[tm - 8:tm]
    pr_ref[...] = raw + (ext[7:7 + tm] - raw) * mu_ref[...]

    c0 += R_COLS
    raw = jnp.dot(ub, w_ref[:, c0:c0 + H_COLS], preferred_element_type=F32)
    lb = lb_ref[...]
    z = raw[:, W_MIX:2 * W_MIX]
    bb = jnp.log1p(-lb) + _log_sigmoid(z)
    log_lb = jnp.log(jnp.maximum(lb, LB_FLOOR))
    lf = jnp.maximum(log_lb, bb) + jnp.log(1.0 + jnp.exp(-jnp.abs(log_lb - bb)))
    ph_ref[:, 0:W_MIX] = jnp.where(valid, _silu(raw[:, 0:W_MIX]), 0.0)
    ph_ref[:, W_MIX:2 * W_MIX] = jnp.where(valid, lf, 0.0)
    ph_ref[:, 2 * W_MIX:3 * W_MIX] = jnp.where(valid, (1.0 - lb) * _sigmoid(-z), 0.0)
    ph_ref[:, 3 * W_MIX:4 * W_MIX] = raw[:, 2 * W_MIX:3 * W_MIX]
    ph_ref[:, 4 * W_MIX:5 * W_MIX] = _silu(raw[:, 3 * W_MIX:4 * W_MIX])

    c0 += H_COLS
    ps_ref[...] = jnp.dot(ub, w_ref[:, c0:c0 + S_COLS], preferred_element_type=F32)


def _proj_in(h, g, w, conv_w, conv_b, gate_b, mu, lb, tm):
    bsz, tp, d = h.shape
    ncols = w.shape[1]
    row = lambda n: pl.BlockSpec((None, tm, n), lambda b, i: (b, i, 0))
    return pl.pallas_call(
        functools.partial(_proj_in_kernel, tm=tm),
        grid=(bsz, tp // tm),
        in_specs=[row(d), _const_spec((1, d)), _const_spec((d, ncols)),
                  _const_spec((MLSTM_CONV, 2 * W_MIX)), _const_spec((1, 2 * W_MIX)),
                  _const_spec((2, W_MIX)), _const_spec((1, R_COLS)), _const_spec((1, W_MIX))],
        out_specs=[row(PM_COLS), row(R_COLS), row(PH_COLS),
                   pl.BlockSpec((tm, S_COLS), lambda b, i: (i, b))],
        out_shape=[jax.ShapeDtypeStruct((bsz, tp, PM_COLS), F32),
                   jax.ShapeDtypeStruct((bsz, tp, R_COLS), F32),
                   jax.ShapeDtypeStruct((bsz, tp, PH_COLS), F32),
                   jax.ShapeDtypeStruct((tp, bsz * S_COLS), F32)],
        scratch_shapes=[pltpu.VMEM((8, 2 * W_MIX + R_COLS), F32)],
        compiler_params=_params(("parallel", "arbitrary")),
        name="proj_in",
    )(h, g, w, conv_w, conv_b, gate_b, mu, lb)


def _mlstm_kernel(pm_ref, ng_ref, y_ref, c_sc, n_sc, m_sc, *, tb, nseq):
    blk = pl.program_id(1)

    @pl.when(blk == 0)
    def _():
        c_sc[...] = jnp.zeros_like(c_sc)
        n_sc[...] = jnp.zeros_like(n_sc)
        m_sc[...] = jnp.zeros_like(m_sc)

    bd2, bones, t_io, s_io, tri = _head_consts()
    causal = t_io >= s_io
    eye4 = (t_io == s_io).astype(F32)
    ones64 = jnp.ones((CHUNK, CHUNK), BF16)
    head_of_lane = _iota((CHUNK, W_MIX), 1) >> 6
    ng = ng_ref[...]

    def chunk_seq(r0, sq):
        cur = pm_ref[sq, pl.ds(r0, CHUNK), :]
        q = cur[:, 0:W_MIX]
        k = cur[:, W_MIX:2 * W_MIX]
        v = cur[:, 2 * W_MIX:3 * W_MIX]
        o_gate = cur[:, 3 * W_MIX:4 * W_MIX]
        li = cur[:, 4 * W_MIX:5 * W_MIX]
        lf = cur[:, 5 * W_MIX:6 * W_MIX]

        c_st = [c_sc[sq, 0], c_sc[sq, 1]]
        n_st = n_sc[sq]
        m_st = m_sc[sq]

        b = _dot01(tri, lf)
        yield
        b_row = _dot01(ones64, b * eye4)
        li_row = _dot01(ones64, li * eye4)
        yield
        dmat = jnp.where(causal, b - b_row + li_row, NEG)
        mx = jnp.full((CHUNK, W_MIX), -3.0e38, F32)
        for h in range(N_HEADS):
            sel = head_of_lane == h
            mh = jnp.max(jnp.where(sel, dmat, -3.0e38), axis=-1, keepdims=True)
            mx = jnp.where(sel, mh, mx)
        inter = b + m_st
        m = jnp.maximum(inter, mx)
        wmat = jnp.exp(dmat - m)
        sc = jnp.exp(inter - m)
        s = _hmm_nt(q, k, bd2) * wmat
        q_c = _state_dot(q, c_st)
        q_n = yield ("head_sum", q * n_st)
        num = sc * q_c + _hmm(s, v, bd2)
        den = sc * q_n + (yield ("head_sum", s))
        hh = num / jnp.maximum(jnp.abs(den), jnp.exp(-m))

        b_last = b[CHUNK - 1:CHUNK]
        g = b_last - b + li
        m_new = jnp.maximum(b_last + m_st, jnp.max(g, axis=0, keepdims=True))
        ws = jnp.exp(g - m_new)
        dec = jnp.exp(b_last + m_st - m_new)
        kw = k * ws
        kv = _outer(kw, v, bd2)
        ms = (yield ("head_sum", hh * hh)) * (1.0 / HEAD_DIM)
        for p in range(2):
            c_sc[sq, p] = dec[:, _lanes(p)] * c_st[p] + kv[p]
        n_sc[sq] = dec * n_st + jnp.sum(kw, axis=0, keepdims=True)
        m_sc[sq] = m_new
        y_ref[sq, pl.ds(r0, CHUNK), :] = o_gate * (hh * lax.rsqrt(ms + RMS_EPS) * ng)

    shared = {"head_sum": lambda x: _head_sum(x, bones)}

    def chunk(c, carry):
        r0 = pl.multiple_of(c * CHUNK, CHUNK)
        _round_robin((chunk_seq(r0, sq) for sq in range(nseq)), shared)
        return carry

    lax.fori_loop(0, tb // CHUNK, chunk, 0)


def _mlstm(pm, norm_g, tb, nseq):
    bsz, tp, _ = pm.shape
    return pl.pallas_call(
        functools.partial(_mlstm_kernel, tb=tb, nseq=nseq),
        grid=(bsz // nseq, tp // tb),
        in_specs=[pl.BlockSpec((nseq, tb, PM_COLS), lambda b, i: (b, i, 0)), _const_spec((1, W_MIX))],
        out_specs=pl.BlockSpec((nseq, tb, W_MIX), lambda b, i: (b, i, 0)),
        out_shape=jax.ShapeDtypeStruct((bsz, tp, W_MIX), F32),
        scratch_shapes=[pltpu.VMEM((nseq, 2, PAIR, PAIR), F32), pltpu.VMEM((nseq, 1, W_MIX), F32),
                        pltpu.VMEM((nseq, 1, W_MIX), F32)],
        compiler_params=_params(("parallel", "arbitrary")),
        name="mlstm",
    )(pm, norm_g)


def _hgrn2_kernel(ph_ref, ng_ref, y_ref, s_sc, *, tb, nseq):
    blk = pl.program_id(1)

    @pl.when(blk == 0)
    def _():
        s_sc[...] = jnp.zeros_like(s_sc)

    bd2, bones, _, _, tri = _head_consts()
    t16 = _iota((SUB, W_MIX), 0)
    ng = ng_ref[...]

    def chunk_seq(r0, sq):
        cur = ph_ref[sq, pl.ds(r0, CHUNK), :]
        q = cur[:, 0:W_MIX]
        lf = cur[:, W_MIX:2 * W_MIX]
        k = cur[:, 2 * W_MIX:3 * W_MIX]
        v = cur[:, 3 * W_MIX:4 * W_MIX]
        gate = cur[:, 4 * W_MIX:5 * W_MIX]
        g = _dot01(tri, lf)
        yield

        st = [s_sc[sq, 0], s_sc[sq, 1]]
        outs = []
        for i in range(CHUNK // SUB):
            lo = i * SUB
            gl = g[lo:lo + SUB] if i == 0 else g[lo:lo + SUB] - g[lo - 1:lo]
            qi, ki, vi, fi = q[lo:lo + SUB], k[lo:lo + SUB], v[lo:lo + SUB], jnp.exp(lf[lo:lo + SUB])
            g_end = gl[SUB - 1:SUB]
            o = _state_dot_nt(qi * jnp.exp(gl), st)
            rows = [None] * SUB
            qb, kb, fb = qi.astype(BF16), ki.astype(BF16), fi.astype(BF16)
            pw = jnp.where(t16 == SUB - 1, qb, jnp.zeros((), BF16))
            rows[SUB - 1] = pw * kb[SUB - 1:SUB]
            for s in range(SUB - 2, -1, -1):
                pw = jnp.where(t16 == s, qb, pw * fb[s + 1:s + 2])
                rows[s] = pw * kb[s:s + 1]
            kt = ki * jnp.exp(g_end - gl)
            vk = _outer(vi, kt, bd2)
            att = yield ("head_sum", jnp.concatenate(rows, axis=0))
            for s in range(SUB):
                o = o + att[s * SUB:(s + 1) * SUB] * vi[s:s + 1]
            outs.append(o)
            e_end = jnp.exp(g_end)
            st = [e_end[:, _lanes(p)] * st[p] + vk[p] for p in range(2)]
        for p in range(2):
            s_sc[sq, p] = st[p]
        o = jnp.concatenate(outs, axis=0)
        ms = (yield ("head_sum", o * o)) * (1.0 / HEAD_DIM)
        y_ref[sq, pl.ds(r0, CHUNK), :] = o * lax.rsqrt(ms + RMS_EPS) * ng * gate

    shared = {"head_sum": lambda x: _head_sum(x, bones)}

    def chunk(c, carry):
        r0 = pl.multiple_of(c * CHUNK, CHUNK)
        _round_robin((chunk_seq(r0, sq) for sq in range(nseq)), shared)
        return carry

    lax.fori_loop(0, tb // CHUNK, chunk, 0)


def _hgrn2(ph, norm_g, tb, nseq):
    bsz, tp, _ = ph.shape
    return pl.pallas_call(
        functools.partial(_hgrn2_kernel, tb=tb, nseq=nseq),
        grid=(bsz // nseq, tp // tb),
        in_specs=[pl.BlockSpec((nseq, tb, PH_COLS), lambda b, i: (b, i, 0)), _const_spec((1, W_MIX))],
        out_specs=pl.BlockSpec((nseq, tb, W_MIX), lambda b, i: (b, i, 0)),
        out_shape=jax.ShapeDtypeStruct((bsz, tp, W_MIX), F32),
        scratch_shapes=[pltpu.VMEM((nseq, 2, PAIR, PAIR), F32)],
        compiler_params=_params(("parallel", "arbitrary")),
        name="hgrn2",
    )(ph, norm_g)


def _rwkv7_kernel(*refs, tb, first, nseq):
    if first:
        (pr_ref, wc_ref, vec_ref, y_ref, vf_out_ref, s_sc) = refs
    else:
        (pr_ref, vf_ref, wc_ref, vec_ref, v1_ref, v2_ref, y_ref, s_sc) = refs
    blk = pl.program_id(1)

    @pl.when(blk == 0)
    def _():
        s_sc[...] = jnp.zeros_like(s_sc)

    bd2, bones, t_io, s_io, tri = _head_consts()
    causal = t_io >= s_io
    strict = t_io > s_io
    eye4 = (t_io == s_io).astype(F32)
    lane = _iota((CHUNK, W_MIX), 1)
    vec = vec_ref[...]
    w0, a0, k_k, k_a, r_k, gn_w, gn_b, v0 = [vec[i:i + 1] for i in range(8)]

    def hmm(x, y):
        return _hmm(x, y, bd2)

    def chunk_seq(r0, sq):
        xm = pr_ref[sq, pl.ds(r0, CHUNK), :]
        r = xm[:, 0:W_MIX]
        k = xm[:, W_MIX:2 * W_MIX]
        v = xm[:, 2 * W_MIX:3 * W_MIX]
        cc = xm[:, 3 * W_MIX:4 * W_MIX]
        feat = jnp.where(lane < RANK_W, jnp.tanh(cc),
                         jnp.where(lane < RANK_W + RANK_A, cc, _sigmoid(cc)))
        proj = yield ("low_rank", feat)
        w_raw = -_softplus(-(w0 + proj[:, 0:W_MIX])) - 0.5
        a_gate = _sigmoid(a0 + proj[:, W_MIX:2 * W_MIX])
        g_out = proj[:, 2 * W_MIX:3 * W_MIX]
        if first:
            vf_out_ref[sq, pl.ds(r0, CHUNK), :] = v
        else:
            vf = vf_ref[sq, pl.ds(r0, CHUNK), :]
            vlow = yield ("v_down", v)
            mix = _sigmoid(v0 + (yield ("v_up", vlow)))
            v = v + (vf - v) * mix
        kk = k * k_k
        lw = -jnp.exp(w_raw)
        cum = _dot01(tri, lw)
        kk_sq = yield ("head_sum", kk * kk)
        kk = kk / jnp.maximum(jnp.sqrt(kk_sq), L2_EPS)
        k = k * (1.0 + (a_gate - 1.0) * k_a)
        a = -kk
        b = kk * a_gate

        tot = cum[CHUNK - 1:CHUNK]
        e_neg = jnp.exp(-cum)
        e_end = jnp.exp(tot - cum)
        a_t = a * jnp.exp(cum - lw)
        r_t = r * jnp.exp(cum)
        ar = jnp.concatenate([a_t, r_t], axis=0)
        nb = _hmm_nt(ar, b * e_neg, bd2)
        nk = _hmm_nt(ar, k * e_neg, bd2)
        yield
        n_ab = jnp.where(strict, nb[0:CHUNK], 0.0)
        m_rb = jnp.where(causal, nb[CHUNK:], 0.0)
        n_ak = jnp.where(strict, nk[0:CHUNK], 0.0)
        m_rk = jnp.where(causal, nk[CHUNK:], 0.0)

        pw = n_ab
        tinv = eye4 + pw
        s_st = [s_sc[sq, 0], s_sc[sq, 1]]
        both = hmm(jnp.concatenate([n_ak, m_rk], axis=0), v)
        nv = both[0:CHUNK]
        y = _state_dot_nt(r_t, s_st) + both[CHUNK:]
        vk = _outer(v, k * e_end, bd2)
        bonus = yield ("head_sum", r * k * r_k)
        for i in range(5):
            yield
            if i == 0:
                pw_sq = hmm(pw, pw)
            else:
                both = hmm(jnp.concatenate([pw, tinv], axis=0), pw)
                pw_sq = both[0:CHUNK]
                tinv = tinv + both[CHUNK:]
            pw = pw_sq
        yield
        tinv = tinv + hmm(tinv, pw)
        yield
        a_hat = hmm(tinv, a_t)
        v_hat = hmm(tinv, nv)
        yield
        u = _state_dot_nt(a_hat, s_st) + v_hat
        yield
        y = y + hmm(m_rb, u)
        ub = _outer(u, b * e_end, bd2)
        yield
        e_tot = jnp.exp(tot)
        for p in range(2):
            s_sc[sq, p] = s_st[p] * e_tot[:, _lanes(p)] + ub[p] + vk[p]
        mean = (yield ("head_sum", y)) * (1.0 / HEAD_DIM)
        yc = y - mean
        var = (yield ("head_sum", yc * yc)) * (1.0 / HEAD_DIM)
        yn = yc * lax.rsqrt(var + GN_EPS) * gn_w + gn_b
        y_ref[sq, pl.ds(r0, CHUNK), :] = (yn + bonus * v) * g_out

    shared = {"head_sum": lambda x: _head_sum(x, bones), "low_rank": lambda x: _dot(x, wc_ref[...])}
    if not first:
        shared["v_down"] = lambda x: _dot(x, v1_ref[...])
        shared["v_up"] = lambda x: _dot(x, v2_ref[...])

    def chunk(c, carry):
        r0 = pl.multiple_of(c * CHUNK, CHUNK)
        _round_robin((chunk_seq(r0, sq) for sq in range(nseq)), shared)
        return carry

    lax.fori_loop(0, tb // CHUNK, chunk, 0)


def _rwkv7(pr, v_first, wc, vec, v1, v2, tb, nseq):
    bsz, tp, _ = pr.shape
    first = v_first is None
    blk = lambda n: pl.BlockSpec((nseq, tb, n), lambda b, i: (b, i, 0))
    yshape = jax.ShapeDtypeStruct((bsz, tp, W_MIX), F32)
    common = [_const_spec((W_MIX, 3 * W_MIX)), _const_spec((8, W_MIX))]
    if first:
        in_specs = [blk(R_COLS)] + common
        args = (pr, wc, vec)
        out_specs, out_shape = [blk(W_MIX), blk(W_MIX)], [yshape, yshape]
    else:
        in_specs = [blk(R_COLS), blk(W_MIX)] + common + [_const_spec((W_MIX, RANK_V)),
                                                          _const_spec((RANK_V, W_MIX))]
        args = (pr, v_first, wc, vec, v1, v2)
        out_specs, out_shape = blk(W_MIX), yshape
    out = pl.pallas_call(
        functools.partial(_rwkv7_kernel, tb=tb, first=first, nseq=nseq),
        grid=(bsz // nseq, tp // tb),
        in_specs=in_specs, out_specs=out_specs, out_shape=out_shape,
        scratch_shapes=[pltpu.VMEM((nseq, 2, PAIR, PAIR), F32)],
        compiler_params=_params(("parallel", "arbitrary")),
        name="rwkv7_first" if first else "rwkv7",
    )(*args)
    return (out[0], out[1]) if first else (out, v_first)


def _s5_disc_kernel(are_ref, aim_ref, ls_ref, bre_ref, bim_ref, ar_ref, ai_ref, bbre_ref, bbim_ref):
    a_re = are_ref[...]
    a_im = aim_ref[...]
    step = jnp.exp(ls_ref[...])
    mag = jnp.exp(a_re * step)
    ang = a_im * step
    ab_re = mag * jnp.cos(ang)
    ab_im = mag * jnp.sin(ang)
    den = a_re * a_re + a_im * a_im
    num_re = ab_re - 1.0
    coef_re = (num_re * a_re + ab_im * a_im) / den
    coef_im = (ab_im * a_re - num_re * a_im) / den
    ar_ref[...] = ab_re
    ai_ref[...] = ab_im
    b_re = bre_ref[...]
    b_im = bim_ref[...]
    bbre_ref[...] = coef_re[None] * b_re - coef_im[None] * b_im
    bbim_ref[...] = coef_re[None] * b_im + coef_im[None] * b_re


def _s5_discretize(a_re, a_im, log_step, b_re_t, b_im_t):
    nl = a_re.shape[0]
    gp = pl.BlockSpec((None, S5_GROUPS, S5_STATE), lambda l: (l, 0, 0))
    cgp = pl.BlockSpec((None, S5_GROUP, S5_GROUPS, S5_STATE), lambda l: (l, 0, 0, 0))
    return pl.pallas_call(
        _s5_disc_kernel,
        grid=(nl,),
        in_specs=[gp, gp, pl.BlockSpec((None, S5_GROUPS, 1), lambda l: (l, 0, 0)), cgp, cgp],
        out_specs=[gp, gp, cgp, cgp],
        out_shape=[jax.ShapeDtypeStruct(a_re.shape, F32), jax.ShapeDtypeStruct(a_re.shape, F32),
                   jax.ShapeDtypeStruct(b_re_t.shape, F32), jax.ShapeDtypeStruct(b_re_t.shape, F32)],
        compiler_params=_params(("parallel",)),
        name="s5_discretize",
    )(a_re, a_im, log_step, b_re_t, b_im_t)


def _s5_kernel(u_ref, wbu_ref, ar_ref, ai_ref, wcre_ref, wcim_ref, d_ref, wglu_ref, bglu_ref,
               y_ref, x_sc, st_sc, *, tb, nb):
    @pl.when(pl.program_id(0) == 0)
    def _():
        st_sc[...] = jnp.zeros_like(st_sc)

    u = u_ref[...]
    x_sc[...] = _dot(u, wbu_ref[...])
    ar = jnp.broadcast_to(ar_ref[...], (nb, S5_N))
    ai = jnp.broadcast_to(ai_ref[...], (nb, S5_N))

    def step(t, carry):
        xr, xi = carry
        rows = pl.ds(pl.multiple_of(t * nb, nb), nb)
        nr = ar * xr - ai * xi + x_sc[rows, 0:S5_N]
        ni = ar * xi + ai * xr + x_sc[rows, S5_N:2 * S5_N]
        x_sc[rows, 0:S5_N] = nr
        x_sc[rows, S5_N:2 * S5_N] = ni
        return nr, ni

    xr, xi = lax.fori_loop(0, tb, step, (st_sc[:, 0:S5_N], st_sc[:, S5_N:2 * S5_N]), unroll=8)
    st_sc[:, 0:S5_N] = xr
    st_sc[:, S5_N:2 * S5_N] = xi
    y = _dot(x_sc[:, 0:S5_N], wcre_ref[...]) - _dot(x_sc[:, S5_N:2 * S5_N], wcim_ref[...])
    y = _gelu_tanh(y + d_ref[...] * u)
    y_ref[...] = y * _sigmoid(_dot(y, wglu_ref[...]) + bglu_ref[...])


def _s5(u_tb, wbu, ar, ai, wcre, wcim, d_skip, wglu, bglu, nb, tb):
    rows = u_tb.shape[0]
    tp = rows // nb
    return pl.pallas_call(
        functools.partial(_s5_kernel, tb=tb, nb=nb),
        grid=(tp // tb,),
        in_specs=[pl.BlockSpec((tb * nb, W_MIX), lambda i: (i, 0)),
                  _const_spec((W_MIX, 2 * S5_N)), _const_spec((1, S5_N)), _const_spec((1, S5_N)),
                  _const_spec((S5_N, W_MIX)), _const_spec((S5_N, W_MIX)), _const_spec((1, W_MIX)),
                  _const_spec((W_MIX, W_MIX)), _const_spec((1, W_MIX))],
        out_specs=pl.BlockSpec((tb * nb, W_MIX), lambda i: (i, 0)),
        out_shape=jax.ShapeDtypeStruct((rows, W_MIX), F32),
        scratch_shapes=[pltpu.VMEM((tb * nb, 2 * S5_N), F32), pltpu.VMEM((nb, 2 * S5_N), F32)],
        compiler_params=_params(("arbitrary",)),
        name="s5",
    )(u_tb, wbu, ar, ai, wcre, wcim, d_skip, wglu, bglu)


def _merge_kernel(x_ref, ym_ref, yr_ref, yh_ref, ys_ref, g0_ref, g1_ref, wg_ref, bg_ref, wbr_ref,
                  wo_ref, o_ref):
    x = x_ref[...]
    ub = _rmsnorm_rows(x, g0_ref[...]).astype(BF16)
    merged = None
    for n, y_ref in enumerate((ym_ref, yr_ref, yh_ref, ys_ref)):
        gate = _sigmoid(jnp.dot(ub, wg_ref[n], preferred_element_type=F32) + bg_ref[n])
        term = gate * _dot(y_ref[...], wbr_ref[n])
        merged = term if merged is None else merged + term
    o_ref[...] = x + _rmsnorm_rows(_dot(merged, wo_ref[...]), g1_ref[...])


def _merge(h, ym, yr, yh, ys_tm, g0, g1, wg, bg, wbr, wo, tm):
    bsz, tp, d = h.shape
    row = lambda n: pl.BlockSpec((None, tm, n), lambda b, i: (b, i, 0))
    return pl.pallas_call(
        _merge_kernel,
        grid=(bsz, tp // tm),
        in_specs=[row(d), row(W_MIX), row(W_MIX), row(W_MIX),
                  pl.BlockSpec((tm, W_MIX), lambda b, i: (i, b)),
                  _const_spec((1, d)), _const_spec((1, d)),
                  _const_spec((N_BRANCH, d, d)), _const_spec((N_BRANCH, 1, d)),
                  _const_spec((N_BRANCH, W_MIX, d)), _const_spec((d, d))],
        out_specs=row(d),
        out_shape=jax.ShapeDtypeStruct(h.shape, F32),
        compiler_params=_params(("parallel", "parallel")),
        name="merge",
    )(h, ym, yr, yh, ys_tm, g0, g1, wg, bg, wbr, wo)


FFN_COLS = 256


def _ffn_kernel(x_ref, g2_ref, g3_ref, wup_ref, cw_ref, cb_ref, wdn_ref, o_ref, halo_sc, act_sc, *, tm):
    i = pl.program_id(1)

    @pl.when(i == 0)
    def _():
        halo_sc[...] = jnp.zeros_like(halo_sc)

    x = x_ref[...]
    u = _rmsnorm_rows(x, g2_ref[...])
    valid = (i * tm + _iota((tm, 1), 0)) >= PAD_FRONT
    ub = jnp.where(valid, u, 0.0).astype(BF16)

    def conv(c0):
        z = jnp.dot(ub, wup_ref[:, c0:c0 + FFN_COLS], preferred_element_type=F32)
        ext = jnp.concatenate([halo_sc[:, c0:c0 + FFN_COLS], z], axis=0)
        halo_sc[:, c0:c0 + FFN_COLS] = z[tm - 8:tm]
        w = cw_ref[:, c0:c0 + FFN_COLS]
        return (cb_ref[:, c0:c0 + FFN_COLS] + w[2:3] * z + w[1:2] * ext[7:7 + tm]
                + w[0:1] * ext[6:6 + tm])

    for c in range(D_FF // FFN_COLS):
        gate = conv(c * FFN_COLS).astype(BF16)
        val = conv(D_FF + c * FFN_COLS).astype(BF16)
        act_sc[:, c * FFN_COLS:(c + 1) * FFN_COLS] = _gelu_tanh(gate) * val
    down = jnp.dot(act_sc[...], wdn_ref[...], preferred_element_type=F32)
    o_ref[...] = x + _rmsnorm_rows(down, g3_ref[...])


def _ffn(h, g2, g3, wup, conv_w, conv_b, wdn, tm):
    bsz, tp, d = h.shape
    row = pl.BlockSpec((None, tm, d), lambda b, i: (b, i, 0))
    return pl.pallas_call(
        functools.partial(_ffn_kernel, tm=tm),
        grid=(bsz, tp // tm),
        in_specs=[row, _const_spec((1, d)), _const_spec((1, d)), _const_spec((d, 2 * D_FF)),
                  _const_spec((FFN_CONV, 2 * D_FF)), _const_spec((1, 2 * D_FF)),
                  _const_spec((D_FF, d))],
        out_specs=row,
        out_shape=jax.ShapeDtypeStruct(h.shape, F32),
        scratch_shapes=[pltpu.VMEM((8, 2 * D_FF), F32), pltpu.VMEM((tm, D_FF), BF16)],
        compiler_params=_params(("parallel", "arbitrary")),
        name="conv_ffn",
    )(h, g2, g3, wup, conv_w, conv_b, wdn)


def _block_diag_in(bb):
    eye = jnp.eye(S5_GROUPS, dtype=F32)
    return jnp.einsum('cgp,gh->gchp', bb, eye).reshape(W_MIX, S5_N)


def _block_diag_out(c):
    eye = jnp.eye(S5_GROUPS, dtype=F32)
    return jnp.einsum('gcp,gh->gphc', c, eye).reshape(S5_N, W_MIX)


def kernel(x, meta, norms, w_in, w_gate, b_gate, w_branch, w_out, m_conv_w, m_conv_b, m_gate_b, m_norm,
           r_mu, r_w0, r_w2, r_a0, r_a2, r_g2, r_kk, r_ka, r_rk, r_gn_w, r_gn_b, r_v0, r_v1, r_v2,
           h_lb, h_norm, s_a_re, s_a_im, s_b_re, s_b_im, s_c_re, s_c_im, s_d, s_log_step, s_w_glu, s_b_glu,
           f_up, f_conv_w, f_conv_b, f_down):
    bsz, seq, d = x.shape
    depth = w_in.shape[0]
    t_all = N_META + seq
    tp = PAD_FRONT + t_all
    assert d == D_MODEL and tp % CHUNK == 0
    tm = _row_tile(tp, 688)
    tb = _chunk_block(tp, 192)
    ts5 = _chunk_block(tp, 192)
    nseq = NSEQ if bsz % NSEQ == 0 else 1

    h = jnp.concatenate([jnp.zeros((bsz, PAD_FRONT, d), F32),
                         jnp.broadcast_to(meta.astype(F32)[None], (bsz, N_META, d)), x], axis=1)

    lb_w = jax.nn.softmax(h_lb.astype(F32), axis=0)
    lbs = jnp.cumsum(lb_w, axis=0) - lb_w[0:1]

    ar, ai, bbre, bbim = _s5_discretize(
        s_a_re, s_a_im, s_log_step[..., None],
        jnp.transpose(s_b_re, (0, 3, 1, 2)), jnp.transpose(s_b_im, (0, 3, 1, 2)))

    row = lambda a: a.reshape(1, -1)
    v_first = None
    for l in range(depth):
        w = w_in[l]
        w_r = jnp.concatenate(
            [w[:, 0:4 * W_MIX],
             jnp.repeat(w[:, 4 * W_MIX:4 * W_MIX + N_HEADS], HEAD_DIM, axis=1),
             jnp.repeat(w[:, 4 * W_MIX + N_HEADS:M_COLS], HEAD_DIM, axis=1),
             w[:, M_COLS:]], axis=1).astype(BF16)
        pm, pr, ph, ps_tm = _proj_in(h, row(norms[l, 0]), w_r, m_conv_w[l], row(m_conv_b[l]),
                                     jnp.repeat(m_gate_b[l], HEAD_DIM, axis=1), row(r_mu[l]),
                                     row(lbs[l]), tm)

        ym = _mlstm(pm, row(m_norm[l]), tb, nseq)

        wc = jnp.zeros((W_MIX, 3 * W_MIX), F32)
        wc = wc.at[0:RANK_W, 0:W_MIX].set(r_w2[l])
        wc = wc.at[RANK_W:RANK_W + RANK_A, W_MIX:2 * W_MIX].set(r_a2[l])
        wc = wc.at[RANK_W + RANK_A:, 2 * W_MIX:].set(r_g2[l]).astype(BF16)
        v0 = r_v0[l - 1] if l > 0 else jnp.zeros((W_MIX,), F32)
        vec = jnp.stack([r_w0[l], r_a0[l], r_kk[l], r_ka[l], r_rk[l], r_gn_w[l], r_gn_b[l], v0])
        if l == 0:
            yr, v_first = _rwkv7(pr, None, wc, vec, None, None, tb, nseq)
        else:
            yr, _ = _rwkv7(pr, v_first, wc, vec, r_v1[l - 1].astype(BF16),
                           r_v2[l - 1].astype(BF16), tb, nseq)

        yh = _hgrn2(ph, row(h_norm[l]), tb, nseq)

        wbu = jnp.concatenate([_block_diag_in(bbre[l]), _block_diag_in(bbim[l])], axis=1).astype(BF16)
        ys_tb = _s5(ps_tm.reshape(tp * bsz, W_MIX), wbu, row(ar[l]), row(ai[l]),
                    _block_diag_out(s_c_re[l]).astype(BF16), _block_diag_out(s_c_im[l]).astype(BF16),
                    row(s_d[l]), s_w_glu[l].astype(BF16), row(s_b_glu[l]), bsz, ts5)

        h = _merge(h, ym, yr, yh, ys_tb.reshape(tp, bsz * W_MIX), row(norms[l, 0]), row(norms[l, 1]),
                   w_gate[l].astype(BF16), b_gate[l][:, None, :], w_branch[l].astype(BF16),
                   w_out[l].astype(BF16), tm)
        h = _ffn(h, row(norms[l, 2]), row(norms[l, 3]), f_up[l].astype(BF16), f_conv_w[l],
                 row(f_conv_b[l]), f_down[l].astype(BF16), tm)
    return h[:, PAD_FRONT + N_META:]
```

```python
import functools
import math

import jax
import jax.numpy as jnp
from jax import lax
from jax.experimental import pallas as pl
from jax.experimental.pallas import tpu as pltpu

F32 = jnp.float32
BF16 = jnp.bfloat16

D_MODEL = 1024
N_META = 16
CHUNK = 64
PAD_FRONT = CHUNK - N_META
N_BRANCH = 4
W_MIX = D_MODEL // 4
HEAD_DIM = 64
N_HEADS = W_MIX // HEAD_DIM
MLSTM_CONV = 4
RANK_W = 64
RANK_A = 64
RANK_V = 32
RANK_G = 128
S5_GROUP = 16
S5_GROUPS = W_MIX // S5_GROUP
S5_STATE = 64
S5_N = S5_GROUPS * S5_STATE
D_FF = 256 * ((8 * D_MODEL // 3 + 255) // 256)
FFN_CONV = 3
M_COLS = 4 * W_MIX + 2 * N_HEADS
R_COLS = 3 * W_MIX + RANK_W + RANK_A + RANK_G
H_COLS = 4 * W_MIX
S_COLS = W_MIX
PH_COLS = 5 * W_MIX
PM_COLS = 6 * W_MIX
RMS_EPS = 1e-6
GN_EPS = 64e-5
L2_EPS = 1e-12
LB_FLOOR = 1e-30
NEG = -1e30
SUB = 16
PAIR = 2 * HEAD_DIM
NSEQ = 8

VMEM_LIMIT_BYTES = 56 * 1024 * 1024


def _dot(a, b):
    return jnp.dot(a.astype(BF16), b.astype(BF16), preferred_element_type=F32)


def _dot_nt(a, b):
    return lax.dot_general(a.astype(BF16), b.astype(BF16), (((1,), (1,)), ((), ())),
                           preferred_element_type=F32)


def _dot_tn(a, b):
    return lax.dot_general(a.astype(BF16), b.astype(BF16), (((0,), (0,)), ((), ())),
                           preferred_element_type=F32)


def _dot01(sel, x):
    hi = x.astype(BF16)
    lo = (x - hi.astype(F32)).astype(BF16)
    return (jnp.dot(sel, hi, preferred_element_type=F32)
            + jnp.dot(sel, lo, preferred_element_type=F32))


def _sigmoid(x):
    return 0.5 * jnp.tanh(0.5 * x) + 0.5


def _softplus(x):
    return jnp.maximum(x, 0.0) + jnp.log(1.0 + jnp.exp(-jnp.abs(x)))


def _log_sigmoid(x):
    return -_softplus(-x)


def _silu(x):
    return x * _sigmoid(x)


def _gelu_tanh(x):
    c = math.sqrt(2.0 / math.pi)
    return (0.5 * x) * (1.0 + jnp.tanh(x * (c + (c * 0.044715) * (x * x))))


def _rmsnorm_rows(x, g):
    ms = jnp.mean(x * x, axis=-1, keepdims=True)
    return x * lax.rsqrt(ms + RMS_EPS) * g


def _iota(shape, dim):
    return lax.broadcasted_iota(jnp.int32, shape, dim)


def _head_consts():
    r = _iota((PAIR, PAIR), 0)
    c = _iota((PAIR, PAIR), 1)
    bd2 = (r >> 6) == (c >> 6)
    t = _iota((CHUNK, W_MIX), 0)
    s = _iota((CHUNK, W_MIX), 1) & (HEAD_DIM - 1)
    tri = (_iota((CHUNK, CHUNK), 0) >= _iota((CHUNK, CHUNK), 1)).astype(BF16)
    return bd2, bd2.astype(BF16), t, s, tri


def _lanes(p):
    return slice(p * PAIR, (p + 1) * PAIR)


def _tile2(xp, bd2):
    return jnp.where(bd2, jnp.concatenate([xp, xp], axis=0), jnp.zeros((), BF16))


def _hmm(x, y, bd2):
    xb, yb = x.astype(BF16), y.astype(BF16)
    return jnp.concatenate(
        [jnp.dot(xb[:, _lanes(p)], _tile2(yb[:, _lanes(p)], bd2), preferred_element_type=F32)
         for p in range(2)], axis=1)


def _hmm_nt(x, y, bd2):
    xb, yb = x.astype(BF16), y.astype(BF16)
    return jnp.concatenate(
        [lax.dot_general(xb[:, _lanes(p)], _tile2(yb[:, _lanes(p)], bd2), (((1,), (1,)), ((), ())),
                         preferred_element_type=F32) for p in range(2)], axis=1)


def _state_dot(x, st):
    return jnp.concatenate([_dot(x[:, _lanes(p)], st[p]) for p in range(2)], axis=1)


def _state_dot_nt(x, st):
    return jnp.concatenate([_dot_nt(x[:, _lanes(p)], st[p]) for p in range(2)], axis=1)


def _outer(a, b, bd2):
    return [jnp.where(bd2, _dot_tn(a[:, _lanes(p)], b[:, _lanes(p)]), 0.0) for p in range(2)]


def _head_sum(x, bones2):
    xb = x.astype(BF16)
    return jnp.concatenate(
        [jnp.dot(xb[:, _lanes(p)], bones2, preferred_element_type=F32) for p in range(2)], axis=1)


def _row_tile(tp, target):
    best = None
    for cand in range(16, tp + 1, 16):
        if tp % cand == 0 and cand <= target:
            best = cand
    return best if best is not None else tp


def _chunk_block(tp, target):
    best = CHUNK
    for cand in range(CHUNK, tp + 1, CHUNK):
        if tp % cand == 0 and cand <= target:
            best = cand
    return best


def _const_spec(shape):
    nd = len(shape)
    return pl.BlockSpec(shape, lambda *_: (0,) * nd)


def _round_robin(gens, shared):
    gens = list(gens)
    sends = [None] * len(gens)
    while gens:
        reqs, alive = [], []
        for g, v in zip(gens, sends):
            try:
                reqs.append(g.send(v))
                alive.append(g)
            except StopIteration:
                pass
        gens = alive
        sends = [None] * len(gens)
        if reqs and reqs[0] is not None:
            xs = [r[1] for r in reqs]
            out = shared[reqs[0][0]](jnp.concatenate(xs, axis=0))
            n = xs[0].shape[0]
            sends = [out[i * n:(i + 1) * n] for i in range(len(xs))]


def _params(sem):
    return pltpu.CompilerParams(dimension_semantics=sem, vmem_limit_bytes=VMEM_LIMIT_BYTES)


def _proj_in_kernel(x_ref, g_ref, w_ref, cw_ref, cb_ref, gb_ref, mu_ref, lb_ref,
                    pm_ref, pr_ref, ph_ref, ps_ref, halo_sc, *, tm):
    i = pl.program_id(1)

    @pl.when(i == 0)
    def _():
        halo_sc[...] = jnp.zeros_like(halo_sc)

    x = x_ref[...]
    u = _rmsnorm_rows(x, g_ref[...])
    valid = (i * tm + _iota((tm, 1), 0)) >= PAD_FRONT
    ub = jnp.where(valid, u, 0.0).astype(BF16)

    raw = jnp.dot(ub, w_ref[:, 0:PM_COLS], preferred_element_type=F32)
    qk_raw = raw[:, 0:2 * W_MIX]
    ext = jnp.concatenate([halo_sc[:, 0:2 * W_MIX], qk_raw], axis=0)
    halo_sc[:, 0:2 * W_MIX] = qk_raw[tm - 8:tm]
    cw = cw_ref[...]
    acc = cb_ref[...] + cw[MLSTM_CONV - 1:MLSTM_CONV] * qk_raw
    for j in range(MLSTM_CONV - 1):
        acc = acc + cw[j:j + 1] * ext[5 + j:5 + j + tm]
    qk = jnp.where(valid, _silu(acc), 0.0)
    gb = gb_ref[...]
    pm_ref[:, 0:W_MIX] = qk[:, 0:W_MIX] * (HEAD_DIM ** -0.5)
    pm_ref[:, W_MIX:2 * W_MIX] = qk[:, W_MIX:2 * W_MIX]
    pm_ref[:, 2 * W_MIX:3 * W_MIX] = raw[:, 2 * W_MIX:3 * W_MIX]
    pm_ref[:, 3 * W_MIX:4 * W_MIX] = _sigmoid(raw[:, 3 * W_MIX:4 * W_MIX])
    pm_ref[:, 4 * W_MIX:5 * W_MIX] = jnp.where(valid, raw[:, 4 * W_MIX:5 * W_MIX] + gb[0:1], NEG)
    pm_ref[:, 5 * W_MIX:6 * W_MIX] = jnp.where(
        valid, _log_sigmoid(raw[:, 5 * W_MIX:6 * W_MIX] + gb[1:2]), 0.0)

    c0 = PM_COLS
    raw = jnp.dot(ub, w_ref[:, c0:c0 + R_COLS], preferred_element_type=F32)
    ext = jnp.concatenate([halo_sc[:, 2 * W_MIX:2 * W_MIX + R_COLS], raw], axis=0)
    halo_sc[:, 2 * W_MIX:2 * W_MIX + R_COLS] = raw[tm - 8:tm]
    pr_ref[...] = raw + (ext[7:7 + tm] - raw) * mu_ref[...]

    c0 += R_COLS
    raw = jnp.dot(ub, w_ref[:, c0:c0 + H_COLS], preferred_element_type=F32)
    lb = lb_ref[...]
    z = raw[:, W_MIX:2 * W_MIX]
    bb = jnp.log1p(-lb) + _log_sigmoid(z)
    log_lb = jnp.log(jnp.maximum(lb, LB_FLOOR))
    lf = jnp.maximum(log_lb, bb) + jnp.log(1.0 + jnp.exp(-jnp.abs(log_lb - bb)))
    ph_ref[:, 0:W_MIX] = jnp.where(valid, _silu(raw[:, 0:W_MIX]), 0.0)
    ph_ref[:, W_MIX:2 * W_MIX] = jnp.where(valid, lf, 0.0)
    ph_ref[:, 2 * W_MIX:3 * W_MIX] = jnp.where(valid, (1.0 - lb) * _sigmoid(-z), 0.0)
    ph_ref[:, 3 * W_MIX:4 * W_MIX] = raw[:, 2 * W_MIX:3 * W_MIX]
    ph_ref[:, 4 * W_MIX:5 * W_MIX] = _silu(raw[:, 3 * W_MIX:4 * W_MIX])

    c0 += H_COLS
    ps_ref[...] = jnp.dot(ub, w_ref[:, c0:c0 + S_COLS], preferred_element_type=F32)


def _proj_in(h, g, w, conv_w, conv_b, gate_b, mu, lb, tm):
    bsz, tp, d = h.shape
    ncols = w.shape[1]
    row = lambda n: pl.BlockSpec((None, tm, n), lambda b, i: (b, i, 0))
    return pl.pallas_call(
        functools.partial(_proj_in_kernel, tm=tm),
        grid=(bsz, tp // tm),
        in_specs=[row(d), _const_spec((1, d)), _const_spec((d, ncols)),
                  _const_spec((MLSTM_CONV, 2 * W_MIX)), _const_spec((1, 2 * W_MIX)),
                  _const_spec((2, W_MIX)), _const_spec((1, R_COLS)), _const_spec((1, W_MIX))],
        out_specs=[row(PM_COLS), row(R_COLS), row(PH_COLS),
                   pl.BlockSpec((tm, S_COLS), lambda b, i: (i, b))],
        out_shape=[jax.ShapeDtypeStruct((bsz, tp, PM_COLS), F32),
                   jax.ShapeDtypeStruct((bsz, tp, R_COLS), F32),
                   jax.ShapeDtypeStruct((bsz, tp, PH_COLS), F32),
                   jax.ShapeDtypeStruct((tp, bsz * S_COLS), F32)],
        scratch_shapes=[pltpu.VMEM((8, 2 * W_MIX + R_COLS), F32)],
        compiler_params=_params(("parallel", "arbitrary")),
        name="proj_in",
    )(h, g, w, conv_w, conv_b, gate_b, mu, lb)


def _mlstm_kernel(pm_ref, ng_ref, y_ref, c_sc, n_sc, m_sc, *, tb, nseq):
    blk = pl.program_id(1)

    @pl.when(blk == 0)
    def _():
        c_sc[...] = jnp.zeros_like(c_sc)
        n_sc[...] = jnp.zeros_like(n_sc)
        m_sc[...] = jnp.zeros_like(m_sc)

    bd2, bones, t_io, s_io, tri = _head_consts()
    causal = t_io >= s_io
    eye4 = (t_io == s_io).astype(F32)
    ones64 = jnp.ones((CHUNK, CHUNK), BF16)
    head_of_lane = _iota((CHUNK, W_MIX), 1) >> 6
    ng = ng_ref[...]

    def chunk_seq(r0, sq):
        cur = pm_ref[sq, pl.ds(r0, CHUNK), :]
        q = cur[:, 0:W_MIX]
        k = cur[:, W_MIX:2 * W_MIX]
        v = cur[:, 2 * W_MIX:3 * W_MIX]
        o_gate = cur[:, 3 * W_MIX:4 * W_MIX]
        li = cur[:, 4 * W_MIX:5 * W_MIX]
        lf = cur[:, 5 * W_MIX:6 * W_MIX]

        c_st = [c_sc[sq, 0], c_sc[sq, 1]]
        n_st = n_sc[sq]
        m_st = m_sc[sq]

        b = _dot01(tri, lf)
        yield
        b_row = _dot01(ones64, b * eye4)
        li_row = _dot01(ones64, li * eye4)
        yield
        dmat = jnp.where(causal, b - b_row + li_row, NEG)
        mx = jnp.full((CHUNK, W_MIX), -3.0e38, F32)
        for h in range(N_HEADS):
            sel = head_of_lane == h
            mh = jnp.max(jnp.where(sel, dmat, -3.0e38), axis=-1, keepdims=True)
            mx = jnp.where(sel, mh, mx)
        inter = b + m_st
        m = jnp.maximum(inter, mx)
        wmat = jnp.exp(dmat - m)
        sc = jnp.exp(inter - m)
        s = _hmm_nt(q, k, bd2) * wmat
        q_c = _state_dot(q, c_st)
        q_n = yield ("head_sum", q * n_st)
        num = sc * q_c + _hmm(s, v, bd2)
        den = sc * q_n + (yield ("head_sum", s))
        hh = num / jnp.maximum(jnp.abs(den), jnp.exp(-m))

        b_last = b[CHUNK - 1:CHUNK]
        g = b_last - b + li
        m_new = jnp.maximum(b_last + m_st, jnp.max(g, axis=0, keepdims=True))
        ws = jnp.exp(g - m_new)
        dec = jnp.exp(b_last + m_st - m_new)
        kw = k * ws
        kv = _outer(kw, v, bd2)
        ms = (yield ("head_sum", hh * hh)) * (1.0 / HEAD_DIM)
        for p in range(2):
            c_sc[sq, p] = dec[:, _lanes(p)] * c_st[p] + kv[p]
        n_sc[sq] = dec * n_st + jnp.sum(kw, axis=0, keepdims=True)
        m_sc[sq] = m_new
        y_ref[sq, pl.ds(r0, CHUNK), :] = o_gate * (hh * lax.rsqrt(ms + RMS_EPS) * ng)

    shared = {"head_sum": lambda x: _head_sum(x, bones)}

    def chunk(c, carry):
        r0 = pl.multiple_of(c * CHUNK, CHUNK)
        _round_robin((chunk_seq(r0, sq) for sq in range(nseq)), shared)
        return carry

    lax.fori_loop(0, tb // CHUNK, chunk, 0)


def _mlstm(pm, norm_g, tb, nseq):
    bsz, tp, _ = pm.shape
    return pl.pallas_call(
        functools.partial(_mlstm_kernel, tb=tb, nseq=nseq),
        grid=(bsz // nseq, tp // tb),
        in_specs=[pl.BlockSpec((nseq, tb, PM_COLS), lambda b, i: (b, i, 0)), _const_spec((1, W_MIX))],
        out_specs=pl.BlockSpec((nseq, tb, W_MIX), lambda b, i: (b, i, 0)),
        out_shape=jax.ShapeDtypeStruct((bsz, tp, W_MIX), F32),
        scratch_shapes=[pltpu.VMEM((nseq, 2, PAIR, PAIR), F32), pltpu.VMEM((nseq, 1, W_MIX), F32),
                        pltpu.VMEM((nseq, 1, W_MIX), F32)],
        compiler_params=_params(("parallel", "arbitrary")),
        name="mlstm",
    )(pm, norm_g)


def _hgrn2_kernel(ph_ref, ng_ref, y_ref, s_sc, *, tb, nseq):
    blk = pl.program_id(1)

    @pl.when(blk == 0)
    def _():
        s_sc[...] = jnp.zeros_like(s_sc)

    bd2, bones, _, _, tri = _head_consts()
    t16 = _iota((SUB, W_MIX), 0)
    ng = ng_ref[...]

    def chunk_seq(r0, sq):
        cur = ph_ref[sq, pl.ds(r0, CHUNK), :]
        q = cur[:, 0:W_MIX]
        lf = cur[:, W_MIX:2 * W_MIX]
        k = cur[:, 2 * W_MIX:3 * W_MIX]
        v = cur[:, 3 * W_MIX:4 * W_MIX]
        gate = cur[:, 4 * W_MIX:5 * W_MIX]
        g = _dot01(tri, lf)
        yield

        st = [s_sc[sq, 0], s_sc[sq, 1]]
        outs = []
        for i in range(CHUNK // SUB):
            lo = i * SUB
            gl = g[lo:lo + SUB] if i == 0 else g[lo:lo + SUB] - g[lo - 1:lo]
            qi, ki, vi, fi = q[lo:lo + SUB], k[lo:lo + SUB], v[lo:lo + SUB], jnp.exp(lf[lo:lo + SUB])
            g_end = gl[SUB - 1:SUB]
            o = _state_dot_nt(qi * jnp.exp(gl), st)
            rows = [None] * SUB
            qb, kb, fb = qi.astype(BF16), ki.astype(BF16), fi.astype(BF16)
            pw = jnp.where(t16 == SUB - 1, qb, jnp.zeros((), BF16))
            rows[SUB - 1] = pw * kb[SUB - 1:SUB]
            for s in range(SUB - 2, -1, -1):
                pw = jnp.where(t16 == s, qb, pw * fb[s + 1:s + 2])
                rows[s] = pw * kb[s:s + 1]
            kt = ki * jnp.exp(g_end - gl)
            vk = _outer(vi, kt, bd2)
            att = yield ("head_sum", jnp.concatenate(rows, axis=0))
            for s in range(SUB):
                o = o + att[s * SUB:(s + 1) * SUB] * vi[s:s + 1]
            outs.append(o)
            e_end = jnp.exp(g_end)
            st = [e_end[:, _lanes(p)] * st[p] + vk[p] for p in range(2)]
        for p in range(2):
            s_sc[sq, p] = st[p]
        o = jnp.concatenate(outs, axis=0)
        ms = (yield ("head_sum", o * o)) * (1.0 / HEAD_DIM)
        y_ref[sq, pl.ds(r0, CHUNK), :] = o * lax.rsqrt(ms + RMS_EPS) * ng * gate

    shared = {"head_sum": lambda x: _head_sum(x, bones)}

    def chunk(c, carry):
        r0 = pl.multiple_of(c * CHUNK, CHUNK)
        _round_robin((chunk_seq(r0, sq) for sq in range(nseq)), shared)
        return carry

    lax.fori_loop(0, tb // CHUNK, chunk, 0)


def _hgrn2(ph, norm_g, tb, nseq):
    bsz, tp, _ = ph.shape
    return pl.pallas_call(
        functools.partial(_hgrn2_kernel, tb=tb, nseq=nseq),
        grid=(bsz // nseq, tp // tb),
        in_specs=[pl.BlockSpec((nseq, tb, PH_COLS), lambda b, i: (b, i, 0)), _const_spec((1, W_MIX))],
        out_specs=pl.BlockSpec((nseq, tb, W_MIX), lambda b, i: (b, i, 0)),
        out_shape=jax.ShapeDtypeStruct((bsz, tp, W_MIX), F32),
        scratch_shapes=[pltpu.VMEM((nseq, 2, PAIR, PAIR), F32)],
        compiler_params=_params(("parallel", "arbitrary")),
        name="hgrn2",
    )(ph, norm_g)


def _rwkv7_kernel(*refs, tb, first, nseq):
    if first:
        (pr_ref, wc_ref, vec_ref, y_ref, vf_out_ref, s_sc) = refs
    else:
        (pr_ref, vf_ref, wc_ref, vec_ref, v1_ref, v2_ref, y_ref, s_sc) = refs
    blk = pl.program_id(1)

    @pl.when(blk == 0)
    def _():
        s_sc[...] = jnp.zeros_like(s_sc)

    bd2, bones, t_io, s_io, tri = _head_consts()
    causal = t_io >= s_io
    strict = t_io > s_io
    eye4 = (t_io == s_io).astype(F32)
    lane = _iota((CHUNK, W_MIX), 1)
    vec = vec_ref[...]
    w0, a0, k_k, k_a, r_k, gn_w, gn_b, v0 = [vec[i:i + 1] for i in range(8)]

    def hmm(x, y):
        return _hmm(x, y, bd2)

    def chunk_seq(r0, sq):
        xm = pr_ref[sq, pl.ds(r0, CHUNK), :]
        r = xm[:, 0:W_MIX]
        k = xm[:, W_MIX:2 * W_MIX]
        v = xm[:, 2 * W_MIX:3 * W_MIX]
        cc = xm[:, 3 * W_MIX:4 * W_MIX]
        feat = jnp.where(lane < RANK_W, jnp.tanh(cc),
                         jnp.where(lane < RANK_W + RANK_A, cc, _sigmoid(cc)))
        proj = yield ("low_rank", feat)
        w_raw = -_softplus(-(w0 + proj[:, 0:W_MIX])) - 0.5
        a_gate = _sigmoid(a0 + proj[:, W_MIX:2 * W_MIX])
        g_out = proj[:, 2 * W_MIX:3 * W_MIX]
        if first:
            vf_out_ref[sq, pl.ds(r0, CHUNK), :] = v
        else:
            vf = vf_ref[sq, pl.ds(r0, CHUNK), :]
            vlow = yield ("v_down", v)
            mix = _sigmoid(v0 + (yield ("v_up", vlow)))
            v = v + (vf - v) * mix
        kk = k * k_k
        lw = -jnp.exp(w_raw)
        cum = _dot01(tri, lw)
        kk_sq = yield ("head_sum", kk * kk)
        kk = kk / jnp.maximum(jnp.sqrt(kk_sq), L2_EPS)
        k = k * (1.0 + (a_gate - 1.0) * k_a)
        a = -kk
        b = kk * a_gate

        tot = cum[CHUNK - 1:CHUNK]
        e_neg = jnp.exp(-cum)
        e_end = jnp.exp(tot - cum)
        a_t = a * jnp.exp(cum - lw)
        r_t = r * jnp.exp(cum)
        ar = jnp.concatenate([a_t, r_t], axis=0)
        nb = _hmm_nt(ar, b * e_neg, bd2)
        nk = _hmm_nt(ar, k * e_neg, bd2)
        yield
        n_ab = jnp.where(strict, nb[0:CHUNK], 0.0)
        m_rb = jnp.where(causal, nb[CHUNK:], 0.0)
        n_ak = jnp.where(strict, nk[0:CHUNK], 0.0)
        m_rk = jnp.where(causal, nk[CHUNK:], 0.0)

        pw = n_ab
        tinv = eye4 + pw
        s_st = [s_sc[sq, 0], s_sc[sq, 1]]
        both = hmm(jnp.concatenate([n_ak, m_rk], axis=0), v)
        nv = both[0:CHUNK]
        y = _state_dot_nt(r_t, s_st) + both[CHUNK:]
        vk = _outer(v, k * e_end, bd2)
        bonus = yield ("head_sum", r * k * r_k)
        for i in range(5):
            yield
            if i == 0:
                pw_sq = hmm(pw, pw)
            else:
                both = hmm(jnp.concatenate([pw, tinv], axis=0), pw)
                pw_sq = both[0:CHUNK]
                tinv = tinv + both[CHUNK:]
            pw = pw_sq
        yield
        tinv = tinv + hmm(tinv, pw)
        yield
        a_hat = hmm(tinv, a_t)
        v_hat = hmm(tinv, nv)
        yield
        u = _state_dot_nt(a_hat, s_st) + v_hat
        yield
        y = y + hmm(m_rb, u)
        ub = _outer(u, b * e_end, bd2)
        yield
        e_tot = jnp.exp(tot)
        for p in range(2):
            s_sc[sq, p] = s_st[p] * e_tot[:, _lanes(p)] + ub[p] + vk[p]
        mean = (yield ("head_sum", y)) * (1.0 / HEAD_DIM)
        yc = y - mean
        var = (yield ("head_sum", yc * yc)) * (1.0 / HEAD_DIM)
        yn = yc * lax.rsqrt(var + GN_EPS) * gn_w + gn_b
        y_ref[sq, pl.ds(r0, CHUNK), :] = (yn + bonus * v) * g_out

    shared = {"head_sum": lambda x: _head_sum(x, bones), "low_rank": lambda x: _dot(x, wc_ref[...])}
    if not first:
        shared["v_down"] = lambda x: _dot(x, v1_ref[...])
        shared["v_up"] = lambda x: _dot(x, v2_ref[...])

    def chunk(c, carry):
        r0 = pl.multiple_of(c * CHUNK, CHUNK)
        _round_robin((chunk_seq(r0, sq) for sq in range(nseq)), shared)
        return carry

    lax.fori_loop(0, tb // CHUNK, chunk, 0)


def _rwkv7(pr, v_first, wc, vec, v1, v2, tb, nseq):
    bsz, tp, _ = pr.shape
    first = v_first is None
    blk = lambda n: pl.BlockSpec((nseq, tb, n), lambda b, i: (b, i, 0))
    yshape = jax.ShapeDtypeStruct((bsz, tp, W_MIX), F32)
    common = [_const_spec((W_MIX, 3 * W_MIX)), _const_spec((8, W_MIX))]
    if first:
        in_specs = [blk(R_COLS)] + common
        args = (pr, wc, vec)
        out_specs, out_shape = [blk(W_MIX), blk(W_MIX)], [yshape, yshape]
    else:
        in_specs = [blk(R_COLS), blk(W_MIX)] + common + [_const_spec((W_MIX, RANK_V)),
                                                          _const_spec((RANK_V, W_MIX))]
        args = (pr, v_first, wc, vec, v1, v2)
        out_specs, out_shape = blk(W_MIX), yshape
    out = pl.pallas_call(
        functools.partial(_rwkv7_kernel, tb=tb, first=first, nseq=nseq),
        grid=(bsz // nseq, tp // tb),
        in_specs=in_specs, out_specs=out_specs, out_shape=out_shape,
        scratch_shapes=[pltpu.VMEM((nseq, 2, PAIR, PAIR), F32)],
        compiler_params=_params(("parallel", "arbitrary")),
        name="rwkv7_first" if first else "rwkv7",
    )(*args)
    return (out[0], out[1]) if first else (out, v_first)


def _s5_disc_kernel(are_ref, aim_ref, ls_ref, bre_ref, bim_ref, ar_ref, ai_ref, bbre_ref, bbim_ref):
    a_re = are_ref[...]
    a_im = aim_ref[...]
    step = jnp.exp(ls_ref[...])
    mag = jnp.exp(a_re * step)
    ang = a_im * step
    ab_re = mag * jnp.cos(ang)
    ab_im = mag * jnp.sin(ang)
    den = a_re * a_re + a_im * a_im
    num_re = ab_re - 1.0
    coef_re = (num_re * a_re + ab_im * a_im) / den
    coef_im = (ab_im * a_re - num_re * a_im) / den
    ar_ref[...] = ab_re
    ai_ref[...] = ab_im
    b_re = bre_ref[...]
    b_im = bim_ref[...]
    bbre_ref[...] = coef_re[None] * b_re - coef_im[None] * b_im
    bbim_ref[...] = coef_re[None] * b_im + coef_im[None] * b_re


def _s5_discretize(a_re, a_im, log_step, b_re_t, b_im_t):
    nl = a_re.shape[0]
    gp = pl.BlockSpec((None, S5_GROUPS, S5_STATE), lambda l: (l, 0, 0))
    cgp = pl.BlockSpec((None, S5_GROUP, S5_GROUPS, S5_STATE), lambda l: (l, 0, 0, 0))
    return pl.pallas_call(
        _s5_disc_kernel,
        grid=(nl,),
        in_specs=[gp, gp, pl.BlockSpec((None, S5_GROUPS, 1), lambda l: (l, 0, 0)), cgp, cgp],
        out_specs=[gp, gp, cgp, cgp],
        out_shape=[jax.ShapeDtypeStruct(a_re.shape, F32), jax.ShapeDtypeStruct(a_re.shape, F32),
                   jax.ShapeDtypeStruct(b_re_t.shape, F32), jax.ShapeDtypeStruct(b_re_t.shape, F32)],
        compiler_params=_params(("parallel",)),
        name="s5_discretize",
    )(a_re, a_im, log_step, b_re_t, b_im_t)


def _s5_kernel(u_ref, wbu_ref, ar_ref, ai_ref, wcre_ref, wcim_ref, d_ref, wglu_ref, bglu_ref,
               y_ref, x_sc, st_sc, *, tb, nb):
    @pl.when(pl.program_id(0) == 0)
    def _():
        st_sc[...] = jnp.zeros_like(st_sc)

    u = u_ref[...]
    x_sc[...] = _dot(u, wbu_ref[...])
    ar = jnp.broadcast_to(ar_ref[...], (nb, S5_N))
    ai = jnp.broadcast_to(ai_ref[...], (nb, S5_N))

    def step(t, carry):
        xr, xi = carry
        rows = pl.ds(pl.multiple_of(t * nb, nb), nb)
        nr = ar * xr - ai * xi + x_sc[rows, 0:S5_N]
        ni = ar * xi + ai * xr + x_sc[rows, S5_N:2 * S5_N]
        x_sc[rows, 0:S5_N] = nr
        x_sc[rows, S5_N:2 * S5_N] = ni
        return nr, ni

    xr, xi = lax.fori_loop(0, tb, step, (st_sc[:, 0:S5_N], st_sc[:, S5_N:2 * S5_N]), unroll=8)
    st_sc[:, 0:S5_N] = xr
    st_sc[:, S5_N:2 * S5_N] = xi
    y = _dot(x_sc[:, 0:S5_N], wcre_ref[...]) - _dot(x_sc[:, S5_N:2 * S5_N], wcim_ref[...])
    y = _gelu_tanh(y + d_ref[...] * u)
    y_ref[...] = y * _sigmoid(_dot(y, wglu_ref[...]) + bglu_ref[...])


def _s5(u_tb, wbu, ar, ai, wcre, wcim, d_skip, wglu, bglu, nb, tb):
    rows = u_tb.shape[0]
    tp = rows // nb
    return pl.pallas_call(
        functools.partial(_s5_kernel, tb=tb, nb=nb),
        grid=(tp // tb,),
        in_specs=[pl.BlockSpec((tb * nb, W_MIX), lambda i: (i, 0)),
                  _const_spec((W_MIX, 2 * S5_N)), _const_spec((1, S5_N)), _const_spec((1, S5_N)),
                  _const_spec((S5_N, W_MIX)), _const_spec((S5_N, W_MIX)), _const_spec((1, W_MIX)),
                  _const_spec((W_MIX, W_MIX)), _const_spec((1, W_MIX))],
        out_specs=pl.BlockSpec((tb * nb, W_MIX), lambda i: (i, 0)),
        out_shape=jax.ShapeDtypeStruct((rows, W_MIX), F32),
        scratch_shapes=[pltpu.VMEM((tb * nb, 2 * S5_N), F32), pltpu.VMEM((nb, 2 * S5_N), F32)],
        compiler_params=_params(("arbitrary",)),
        name="s5",
    )(u_tb, wbu, ar, ai, wcre, wcim, d_skip, wglu, bglu)


def _merge_kernel(x_ref, ym_ref, yr_ref, yh_ref, ys_ref, g0_ref, g1_ref, wg_ref, bg_ref, wbr_ref,
                  wo_ref, o_ref):
    x = x_ref[...]
    ub = _rmsnorm_rows(x, g0_ref[...]).astype(BF16)
    merged = None
    for n, y_ref in enumerate((ym_ref, yr_ref, yh_ref, ys_ref)):
        gate = _sigmoid(jnp.dot(ub, wg_ref[n], preferred_element_type=F32) + bg_ref[n])
        term = gate * _dot(y_ref[...], wbr_ref[n])
        merged = term if merged is None else merged + term
    o_ref[...] = x + _rmsnorm_rows(_dot(merged, wo_ref[...]), g1_ref[...])


def _merge(h, ym, yr, yh, ys_tm, g0, g1, wg, bg, wbr, wo, tm):
    bsz, tp, d = h.shape
    row = lambda n: pl.BlockSpec((None, tm, n), lambda b, i: (b, i, 0))
    return pl.pallas_call(
        _merge_kernel,
        grid=(bsz, tp // tm),
        in_specs=[row(d), row(W_MIX), row(W_MIX), row(W_MIX),
                  pl.BlockSpec((tm, W_MIX), lambda b, i: (i, b)),
                  _const_spec((1, d)), _const_spec((1, d)),
                  _const_spec((N_BRANCH, d, d)), _const_spec((N_BRANCH, 1, d)),
                  _const_spec((N_BRANCH, W_MIX, d)), _const_spec((d, d))],
        out_specs=row(d),
        out_shape=jax.ShapeDtypeStruct(h.shape, F32),
        compiler_params=_params(("parallel", "parallel")),
        name="merge",
    )(h, ym, yr, yh, ys_tm, g0, g1, wg, bg, wbr, wo)


FFN_COLS = 256


def _ffn_kernel(x_ref, g2_ref, g3_ref, wup_ref, cw_ref, cb_ref, wdn_ref, o_ref, halo_sc, act_sc, *, tm):
    i = pl.program_id(1)

    @pl.when(i == 0)
    def _():
        halo_sc[...] = jnp.zeros_like(halo_sc)

    x = x_ref[...]
    u = _rmsnorm_rows(x, g2_ref[...])
    valid = (i * tm + _iota((tm, 1), 0)) >= PAD_FRONT
    ub = jnp.where(valid, u, 0.0).astype(BF16)

    def conv(c0):
        z = jnp.dot(ub, wup_ref[:, c0:c0 + FFN_COLS], preferred_element_type=F32)
        ext = jnp.concatenate([halo_sc[:, c0:c0 + FFN_COLS], z], axis=0)
        halo_sc[:, c0:c0 + FFN_COLS] = z[tm - 8:tm]
        w = cw_ref[:, c0:c0 + FFN_COLS]
        return (cb_ref[:, c0:c0 + FFN_COLS] + w[2:3] * z + w[1:2] * ext[7:7 + tm]
                + w[0:1] * ext[6:6 + tm])

    for c in range(D_FF // FFN_COLS):
        gate = conv(c * FFN_COLS).astype(BF16)
        val = conv(D_FF + c * FFN_COLS).astype(BF16)
        act_sc[:, c * FFN_COLS:(c + 1) * FFN_COLS] = _gelu_tanh(gate) * val
    down = jnp.dot(act_sc[...], wdn_ref[...], preferred_element_type=F32)
    o_ref[...] = x + _rmsnorm_rows(down, g3_ref[...])


def _ffn(h, g2, g3, wup, conv_w, conv_b, wdn, tm):
    bsz, tp, d = h.shape
    row = pl.BlockSpec((None, tm, d), lambda b, i: (b, i, 0))
    return pl.pallas_call(
        functools.partial(_ffn_kernel, tm=tm),
        grid=(bsz, tp // tm),
        in_specs=[row, _const_spec((1, d)), _const_spec((1, d)), _const_spec((d, 2 * D_FF)),
                  _const_spec((FFN_CONV, 2 * D_FF)), _const_spec((1, 2 * D_FF)),
                  _const_spec((D_FF, d))],
        out_specs=row,
        out_shape=jax.ShapeDtypeStruct(h.shape, F32),
        scratch_shapes=[pltpu.VMEM((8, 2 * D_FF), F32), pltpu.VMEM((tm, D_FF), BF16)],
        compiler_params=_params(("parallel", "arbitrary")),
        name="conv_ffn",
    )(h, g2, g3, wup, conv_w, conv_b, wdn)


def _block_diag_in(bb):
    eye = jnp.eye(S5_GROUPS, dtype=F32)
    return jnp.einsum('cgp,gh->gchp', bb, eye).reshape(W_MIX, S5_N)


def _block_diag_out(c):
    eye = jnp.eye(S5_GROUPS, dtype=F32)
    return jnp.einsum('gcp,gh->gphc', c, eye).reshape(S5_N, W_MIX)


def kernel(x, meta, norms, w_in, w_gate, b_gate, w_branch, w_out, m_conv_w, m_conv_b, m_gate_b, m_norm,
           r_mu, r_w0, r_w2, r_a0, r_a2, r_g2, r_kk, r_ka, r_rk, r_gn_w, r_gn_b, r_v0, r_v1, r_v2,
           h_lb, h_norm, s_a_re, s_a_im, s_b_re, s_b_im, s_c_re, s_c_im, s_d, s_log_step, s_w_glu, s_b_glu,
           f_up, f_conv_w, f_conv_b, f_down):
    bsz, seq, d = x.shape
    depth = w_in.shape[0]
    t_all = N_META + seq
    tp = PAD_FRONT + t_all
    assert d == D_MODEL and tp % CHUNK == 0
    tm = _row_tile(tp, 688)
    tb = _chunk_block(tp, 192)
    ts5 = _chunk_block(tp, 192)
    nseq = NSEQ if bsz % NSEQ == 0 else 1

    h = jnp.concatenate([jnp.zeros((bsz, PAD_FRONT, d), F32),
                         jnp.broadcast_to(meta.astype(F32)[None], (bsz, N_META, d)), x], axis=1)

    lb_w = jax.nn.softmax(h_lb.astype(F32), axis=0)
    lbs = jnp.cumsum(lb_w, axis=0) - lb_w[0:1]

    ar, ai, bbre, bbim = _s5_discretize(
        s_a_re, s_a_im, s_log_step[..., None],
        jnp.transpose(s_b_re, (0, 3, 1, 2)), jnp.transpose(s_b_im, (0, 3, 1, 2)))

    row = lambda a: a.reshape(1, -1)
    v_first = None
    for l in range(depth):
        w = w_in[l]
        w_r = jnp.concatenate(
            [w[:, 0:4 * W_MIX],
             jnp.repeat(w[:, 4 * W_MIX:4 * W_MIX + N_HEADS], HEAD_DIM, axis=1),
             jnp.repeat(w[:, 4 * W_MIX + N_HEADS:M_COLS], HEAD_DIM, axis=1),
             w[:, M_COLS:]], axis=1).astype(BF16)
        pm, pr, ph, ps_tm = _proj_in(h, row(norms[l, 0]), w_r, m_conv_w[l], row(m_conv_b[l]),
                                     jnp.repeat(m_gate_b[l], HEAD_DIM, axis=1), row(r_mu[l]),
                                     row(lbs[l]), tm)

        ym = _mlstm(pm, row(m_norm[l]), tb, nseq)

        wc = jnp.zeros((W_MIX, 3 * W_MIX), F32)
        wc = wc.at[0:RANK_W, 0:W_MIX].set(r_w2[l])
        wc = wc.at[RANK_W:RANK_W + RANK_A, W_MIX:2 * W_MIX].set(r_a2[l])
        wc = wc.at[RANK_W + RANK_A:, 2 * W_MIX:].set(r_g2[l]).astype(BF16)
        v0 = r_v0[l - 1] if l > 0 else jnp.zeros((W_MIX,), F32)
        vec = jnp.stack([r_w0[l], r_a0[l], r_kk[l], r_ka[l], r_rk[l], r_gn_w[l], r_gn_b[l], v0])
        if l == 0:
            yr, v_first = _rwkv7(pr, None, wc, vec, None, None, tb, nseq)
        else:
            yr, _ = _rwkv7(pr, v_first, wc, vec, r_v1[l - 1].astype(BF16),
                           r_v2[l - 1].astype(BF16), tb, nseq)

        yh = _hgrn2(ph, row(h_norm[l]), tb, nseq)

        wbu = jnp.concatenate([_block_diag_in(bbre[l]), _block_diag_in(bbim[l])], axis=1).astype(BF16)
        ys_tb = _s5(ps_tm.reshape(tp * bsz, W_MIX), wbu, row(ar[l]), row(ai[l]),
                    _block_diag_out(s_c_re[l]).astype(BF16), _block_diag_out(s_c_im[l]).astype(BF16),
                    row(s_d[l]), s_w_glu[l].astype(BF16), row(s_b_glu[l]), bsz, ts5)

        h = _merge(h, ym, yr, yh, ys_tb.reshape(tp, bsz * W_MIX), row(norms[l, 0]), row(norms[l, 1]),
                   w_gate[l].astype(BF16), b_gate[l][:, None, :], w_branch[l].astype(BF16),
                   w_out[l].astype(BF16), tm)
        h = _ffn(h, row(norms[l, 2]), row(norms[l, 3]), f_up[l].astype(BF16), f_conv_w[l],
                 row(f_conv_b[l]), f_down[l].astype(BF16), tm)
    return h[:, PAD_FRONT + N_META:]
```
